```python
import math, functools
import jax, jax.numpy as jnp
from jax import lax
import numpy as np

D_MODEL = 2048
BATCH = 8
SEQ = 2048
DEPTH = 2
DEC_BATCH = 32
DEC_SEQ = 4
PAST_LEN = 8192
PAGE_SIZE = 128

HEAD_DIM = 64
MIX_A = D_MODEL // 2
H_A = MIX_A // HEAD_DIM
DIL_GROUPS = ((128, 1), (512, 4), (2048, 16))
WIN_MAX = 2048
N_BUCKETS = 32
MAX_DISTANCE = 2048
MIX_B = D_MODEL - MIX_A
H_B = 4
DV_B = MIX_B // H_B
DK_B = DV_B // 2
KEY_B = H_B * DK_B
GATE_RANK = 16
GATE_TAU = 16.0
GLA_CHUNK = 16
D_FF = ((8 * D_MODEL // 3 + 255) // 256) * 256
CONV_W = 3
EPS = 1e-6
ATTN_SCALE = HEAD_DIM ** -0.5
SPLIT_SIZES = (MIX_A, MIX_A, MIX_A, KEY_B, KEY_B, MIX_B, MIX_B, GATE_RANK)
IN_COLS = sum(SPLIT_SIZES)

kernel_name = 'hybrid_dilated_gla_convffn_step'


def rmsnorm(x, g):
    xf = x.astype(jnp.float32)
    y = xf * lax.rsqrt(jnp.mean(xf * xf, axis=-1, keepdims=True) + EPS)
    return (y * g.astype(jnp.float32)).astype(x.dtype)


def t5_bucket(dist):
    exact = N_BUCKETS // 2
    df = jnp.maximum(dist, 1).astype(jnp.float32)
    large = exact + (jnp.log(df / exact) / math.log(MAX_DISTANCE / exact) * (N_BUCKETS - exact)).astype(jnp.int32)
    return jnp.where(dist < exact, dist, jnp.minimum(large, N_BUCKETS - 1))


def masked_softmax_stats(s, valid):
    s = jnp.where(valid, s, -jnp.inf)
    m = jnp.max(s, axis=-1, keepdims=True)
    p = jnp.exp(s - m)
    l = jnp.sum(p, axis=-1, keepdims=True)
    return p / l, (m + jnp.log(l))[..., 0]


def dilated_prompt(q, k, v, rel_bias, window, dil):
    B, S, H, Dh = q.shape
    n = window // dil
    L = S // dil
    NB = -(-L // n)
    Lp = NB * n
    sub = lambda t: t.reshape(B, L, dil, H, Dh)
    qb = jnp.pad(sub(q), ((0, 0), (0, Lp - L), (0, 0), (0, 0), (0, 0))).reshape(B, NB, n, dil, H, Dh)
    pad_kv = lambda t: jnp.pad(sub(t), ((0, 0), (n, Lp - L), (0, 0), (0, 0), (0, 0))).reshape(B, NB + 1, n, dil, H, Dh)
    kb, vb = pad_kv(k), pad_kv(v)
    kcat = jnp.concatenate([kb[:, :-1], kb[:, 1:]], axis=2)
    vcat = jnp.concatenate([vb[:, :-1], vb[:, 1:]], axis=2)
    iq = jnp.arange(n)[:, None]
    ik = jnp.arange(2 * n)[None, :]
    dist = n + iq - ik
    blk = jnp.arange(NB)[:, None, None]
    valid = (dist >= 0) & (dist <= n) & (blk * n + ik - n >= 0)
    bias = jnp.transpose(rel_bias[t5_bucket(jnp.clip(dist, 0, n) * dil)], (2, 0, 1)).astype(jnp.float32)
    s = jnp.einsum('bnqrhd,bnkrhd->bnrhqk', qb, kcat) * ATTN_SCALE + bias
    p, lse = masked_softmax_stats(s, valid[None, :, None, None])
    o = jnp.einsum('bnrhqk,bnkrhd->bnqrhd', p, vcat)
    o = o.reshape(B, Lp, dil, H, Dh)[:, :L].reshape(B, S, H, Dh)
    lse = jnp.transpose(lse, (0, 1, 4, 2, 3)).reshape(B, Lp, dil, H)[:, :L].reshape(B, S, H)
    return o, lse


def dilated_sample(q, k_all, v_all, rel_bias, window, dil, wbuf):
    T = q.shape[1]
    n = window // dil
    steps = jnp.arange(n + 1)
    idx = wbuf + jnp.arange(T)[:, None] - steps[None, :] * dil
    valid = idx >= 0
    idx = jnp.maximum(idx, 0)
    kg, vg = k_all[:, idx], v_all[:, idx]
    bias = rel_bias[t5_bucket(steps * dil)].T.astype(jnp.float32)
    s = jnp.einsum('bthd,btmhd->bhtm', q, kg) * ATTN_SCALE + bias[:, None, :]
    p, lse = masked_softmax_stats(s, valid[None, None])
    o = jnp.einsum('bhtm,btmhd->bthd', p, vg)
    return o, jnp.transpose(lse, (0, 2, 1))


def combine_dilations(results):
    os = jnp.stack([o for o, _ in results], axis=0)
    alpha = jax.nn.softmax(jnp.stack([l for _, l in results], axis=0), axis=0)
    return jnp.sum(alpha[..., None] * os, axis=0)


def mix_a_prompt(q, k, v, rel_bias):
    q, k, v = (t.astype(jnp.float32) for t in (q, k, v))
    return combine_dilations([dilated_prompt(q, k, v, rel_bias, w, d) for (w, d) in DIL_GROUPS])


def mix_a_sample(q, k, v, k_past, v_past, rel_bias):
    wbuf = k_past.shape[1]
    k_all = jnp.concatenate([k_past.astype(jnp.float32), k.astype(jnp.float32)], axis=1)
    v_all = jnp.concatenate([v_past.astype(jnp.float32), v.astype(jnp.float32)], axis=1)
    qf = q.astype(jnp.float32)
    return combine_dilations([dilated_sample(qf, k_all, v_all, rel_bias, w, d, wbuf) for (w, d) in DIL_GROUPS])


def gla_prompt(q, k, v, log_a, s0):
    B, S, H, K = q.shape
    C = GLA_CHUNK
    NC = S // C
    chunks = lambda t: t.astype(jnp.float32).reshape(B, NC, C, H, -1).transpose(1, 0, 3, 2, 4)
    qc, kc, vc, ac = chunks(q), chunks(k), chunks(v), chunks(log_a)
    b = jnp.cumsum(ac, axis=3)
    b_last = b[:, :, :, -1:, :]
    qe = qc * jnp.exp(b)
    ke = kc * jnp.exp(-b)
    kd = kc * jnp.exp(b_last - b)
    causal = jnp.tril(jnp.ones((C, C), dtype=bool))
    att = jnp.where(causal, jnp.einsum('nbhck,nbhsk->nbhcs', qe, ke), 0.0)
    o_intra = jnp.einsum('nbhcs,nbhsv->nbhcv', att, vc)

    def step(state, inp):
        qe_c, kd_c, v_c, dec_c = inp
        o_c = jnp.einsum('bhck,bhkv->bhcv', qe_c, state)
        state = dec_c[:, :, 0, :, None] * state + jnp.einsum('bhck,bhcv->bhkv', kd_c, v_c)
        return state, o_c

    s_fin, o_inter = lax.scan(step, s0.astype(jnp.float32), (qe, kd, vc, jnp.exp(b_last)))
    o = (o_intra + o_inter).transpose(1, 0, 3, 2, 4).reshape(B, S, H, -1)
    return o, s_fin


def gla_sample(q, k, v, log_a, s0):
    xs = tuple(t.astype(jnp.float32).transpose(1, 0, 2, 3) for t in (q, k, v, log_a))

    def step(state, inp):
        q_t, k_t, v_t, a_t = inp
        state = jnp.exp(a_t)[..., None] * state + k_t[..., :, None] * v_t[..., None, :]
        return state, jnp.einsum('bhk,bhkv->bhv', q_t, state)

    s_fin, o = lax.scan(step, s0.astype(jnp.float32), xs)
    return o.transpose(1, 0, 2, 3), s_fin


def mix_b_prompt(q, k, v, log_a):
    B = q.shape[0]
    return gla_prompt(q, k, v, log_a, jnp.zeros((B, H_B, DK_B, DV_B), jnp.float32))


def mix_b_sample(q, k, v, log_a, s0):
    return gla_sample(q, k, v, log_a, s0)


def project(h, w_in, q_norm_g, k_norm_g, w_gate2, b_gate2):
    B, T, _ = h.shape
    offs = [int(o) for o in np.cumsum(SPLIT_SIZES)[:-1]]
    qa, ka, va, qb, kb, vb, r, glr = jnp.split(h @ w_in, offs, axis=-1)
    heads_a = lambda t: t.reshape(B, T, H_A, HEAD_DIM)
    qa = rmsnorm(heads_a(qa), q_norm_g)
    ka = rmsnorm(heads_a(ka), k_norm_g)
    va = heads_a(va)
    qb = qb.reshape(B, T, H_B, DK_B) * (DK_B ** -0.5)
    kb = kb.reshape(B, T, H_B, DK_B)
    vb = vb.reshape(B, T, H_B, DV_B)
    log_a = (jax.nn.log_sigmoid((glr @ w_gate2 + b_gate2).astype(jnp.float32)) / GATE_TAU).reshape(B, T, H_B, DK_B)
    return qa, ka, va, qb, kb, vb, r, log_a


def merge_out(oa, ob, r, a_out_g, gla_out_g, w_out, dtype):
    B, T = oa.shape[:2]
    oa = rmsnorm(oa, a_out_g).reshape(B, T, MIX_A)
    ob = rmsnorm(ob, gla_out_g).reshape(B, T, MIX_B) * jax.nn.silu(r.astype(jnp.float32))
    return jnp.concatenate([oa, ob], axis=-1).astype(dtype) @ w_out


def conv_ffn(h, hist, w_up, conv_w, conv_b, w_down):
    T = h.shape[1]
    g_pre, u = jnp.split(h @ w_up, 2, axis=-1)
    gp = jnp.concatenate([hist.astype(g_pre.dtype), g_pre], axis=1)
    conv = conv_b + conv_w[0] * gp[:, 0:T]
    for i in range(1, CONV_W):
        conv = conv + conv_w[i] * gp[:, i:i + T]
    y = (jax.nn.silu(conv) * u) @ w_down
    return y, gp[:, T:]


def layer(x, c, conv_hist, mix_a, mix_b, lw):
    (norm1_g, norm2_g, w_ada, b_ada, w_in, q_norm_g, k_norm_g, w_gate2, b_gate2,
     a_out_g, gla_out_g, w_out, w_up, conv_w, conv_b, w_down) = lw
    mod = jax.nn.silu(c) @ w_ada + b_ada
    shift1, scale1, gate1, shift2, scale2, gate2 = [t[:, None, :] for t in jnp.split(mod, 6, axis=-1)]
    h = rmsnorm(x, norm1_g) * (1 + scale1) + shift1
    qa, ka, va, qb, kb, vb, r, log_a = project(h, w_in, q_norm_g, k_norm_g, w_gate2, b_gate2)
    oa = mix_a(qa, ka, va)
    ob, gla_state = mix_b(qb, kb, vb, log_a)
    x = x + gate1 * merge_out(oa, ob, r, a_out_g, gla_out_g, w_out, x.dtype)
    h2 = rmsnorm(x, norm2_g) * (1 + scale2) + shift2
    y, new_hist = conv_ffn(h2, conv_hist, w_up, conv_w, conv_b, w_down)
    x = x + gate2 * y
    return x, ka, va, gla_state, new_hist


def setup_inputs(seed: int = 0) -> dict:
    key = jax.random.key(seed)
    ks = jax.random.split(key, 25)
    nrm = lambda i, shape, scale: scale * jax.random.normal(ks[i], shape, jnp.float32)
    gain = lambda i, shape: 1.0 + nrm(i, shape, 0.05)
    wbuf = min(WIN_MAX, PAST_LEN)
    return {
        'x_prompt': nrm(0, (BATCH, SEQ, D_MODEL), 1.0),
        'x_sample': nrm(1, (DEC_BATCH, DEC_SEQ, D_MODEL), 1.0),
        'cache_k_win': nrm(2, (DEPTH, DEC_BATCH, wbuf, H_A, HEAD_DIM), 1.0),
        'cache_v_win': nrm(3, (DEPTH, DEC_BATCH, wbuf, H_A, HEAD_DIM), 1.0),
        'state_gla': nrm(4, (DEPTH, DEC_BATCH, H_B, DK_B, DV_B), 1.0),
        'state_conv': nrm(5, (DEPTH, DEC_BATCH, CONV_W - 1, D_FF), 1.0),
        'c_prompt': nrm(6, (BATCH, D_MODEL), 1.0),
        'c_sample': nrm(7, (DEC_BATCH, D_MODEL), 1.0),
        'rel_bias': nrm(8, (N_BUCKETS, H_A), 0.5),
        'norm1_g': gain(9, (DEPTH, D_MODEL)),
        'norm2_g': gain(10, (DEPTH, D_MODEL)),
        'w_ada': nrm(11, (DEPTH, D_MODEL, 6 * D_MODEL), 0.5 * D_MODEL ** -0.5),
        'b_ada': nrm(12, (DEPTH, 6 * D_MODEL), 0.02),
        'w_in': nrm(13, (DEPTH, D_MODEL, IN_COLS), D_MODEL ** -0.5),
        'q_norm_g': gain(14, (DEPTH, HEAD_DIM)),
        'k_norm_g': gain(15, (DEPTH, HEAD_DIM)),
        'w_gate2': nrm(16, (DEPTH, GATE_RANK, KEY_B), GATE_RANK ** -0.5),
        'b_gate2': nrm(17, (DEPTH, KEY_B), 0.1),
        'a_out_g': gain(18, (DEPTH, HEAD_DIM)),
        'gla_out_g': gain(19, (DEPTH, DV_B)),
        'w_out': nrm(20, (DEPTH, D_MODEL, D_MODEL), D_MODEL ** -0.5),
        'w_up': nrm(21, (DEPTH, D_MODEL, 2 * D_FF), D_MODEL ** -0.5),
        'conv_w': nrm(22, (DEPTH, CONV_W, D_FF), CONV_W ** -0.5),
        'conv_b': nrm(23, (DEPTH, D_FF), 0.02),
        'w_down': nrm(24, (DEPTH, D_FF, D_MODEL), D_FF ** -0.5),
    }


def reference(x_prompt, x_sample, cache_k_win, cache_v_win, state_gla, state_conv, c_prompt, c_sample,
              rel_bias, norm1_g, norm2_g, w_ada, b_ada, w_in, q_norm_g, k_norm_g, w_gate2, b_gate2,
              a_out_g, gla_out_g, w_out, w_up, conv_w, conv_b, w_down):
    S = x_prompt.shape[1]
    keep_p = min(WIN_MAX, S)
    xp, xs = x_prompt, x_sample
    kp_l, vp_l, gp_l, cp_l, ks_l, vs_l, gs_l, cs_l = [], [], [], [], [], [], [], []
    for l in range(DEPTH):
        lw = (norm1_g[l], norm2_g[l], w_ada[l], b_ada[l], w_in[l], q_norm_g[l], k_norm_g[l], w_gate2[l],
              b_gate2[l], a_out_g[l], gla_out_g[l], w_out[l], w_up[l], conv_w[l], conv_b[l], w_down[l])
        hist0 = jnp.zeros((xp.shape[0], CONV_W - 1, D_FF), xp.dtype)
        xp, ka, va, g_st, c_st = layer(xp, c_prompt, hist0,
                                       functools.partial(mix_a_prompt, rel_bias=rel_bias),
                                       mix_b_prompt, lw)
        kp_l.append(ka[:, S - keep_p:])
        vp_l.append(va[:, S - keep_p:])
        gp_l.append(g_st)
        cp_l.append(c_st)
        xs, ka, va, g_st, c_st = layer(xs, c_sample, state_conv[l],
                                       functools.partial(mix_a_sample, k_past=cache_k_win[l],
                                                         v_past=cache_v_win[l], rel_bias=rel_bias),
                                       functools.partial(mix_b_sample, s0=state_gla[l]), lw)
        ks_l.append(ka)
        vs_l.append(va)
        gs_l.append(g_st)
        cs_l.append(c_st)
    return (xp, xs, jnp.stack(kp_l), jnp.stack(vp_l), jnp.stack(gp_l), jnp.stack(cp_l),
            jnp.stack(ks_l), jnp.stack(vs_l), jnp.stack(gs_l), jnp.stack(cs_l))
```

```python
import functools
import math

import jax
import jax.numpy as jnp
from jax import lax
from jax.experimental import pallas as pl
from jax.experimental.pallas import tpu as pltpu

F32 = jnp.float32
BF16 = jnp.bfloat16

D_MODEL = 2048
DEPTH = 2
HEAD_DIM = 64
MIX_A = D_MODEL // 2
H_A = MIX_A // HEAD_DIM
DIL_GROUPS = ((128, 1), (512, 4), (2048, 16))
N_BUCKETS = 32
MAX_DISTANCE = 2048
MIX_B = D_MODEL - MIX_A
H_B = 4
DV_B = MIX_B // H_B
DK_B = DV_B // 2
KEY_B = H_B * DK_B
GATE_RANK = 16
GATE_TAU = 16.0
D_FF = ((8 * D_MODEL // 3 + 255) // 256) * 256
CONV_W = 3
EPS = 1e-6
ATTN_SCALE = HEAD_DIM ** -0.5
MAIN_COLS = 3 * MIX_A + 2 * KEY_B + 2 * MIX_B
BAND = 128
NEG = -1e30

LANES = 128
SUBLANES = 8
VMEM_LIMIT = 56 * 1024 * 1024
GLA_BLOCK = 128
GLA_SUB = 16

_NT = (((1,), (1,)), ((), ()))


def _cparams(*sem):
    return pltpu.CompilerParams(dimension_semantics=sem, vmem_limit_bytes=VMEM_LIMIT)


def _dot(a, b):
    return jnp.dot(a, b, preferred_element_type=F32)


def _dot_nt(a, b):
    return lax.dot_general(a, b, _NT, preferred_element_type=F32)


def _silu(x):
    return x * jax.nn.sigmoid(x)


def _rms(x, g):
    return x * lax.rsqrt(jnp.mean(x * x, axis=-1, keepdims=True) + EPS) * g


def _ada_kernel(c_ref, w_ref, b_ref, o_ref):
    a = _silu(c_ref[...]).astype(BF16)
    o_ref[...] = _dot(a, w_ref[...].astype(BF16)) + b_ref[...]


def _ada_mods(c_all, w_ada, b_ada):
    G = c_all.shape[0]
    tn = 1024
    return pl.pallas_call(
        _ada_kernel,
        grid=(DEPTH, 6 * D_MODEL // tn),
        in_specs=[pl.BlockSpec((G, D_MODEL), lambda l, j: (0, 0)),
                  pl.BlockSpec((None, D_MODEL, tn), lambda l, j: (l, 0, j)),
                  pl.BlockSpec((None, 1, tn), lambda l, j: (l, 0, j))],
        out_specs=pl.BlockSpec((None, G, tn), lambda l, j: (l, 0, j)),
        out_shape=jax.ShapeDtypeStruct((DEPTH, G, 6 * D_MODEL), F32),
        compiler_params=_cparams("parallel", "parallel"),
        name="ada_mods",
    )(c_all, w_ada, b_ada.reshape(DEPTH, 1, 6 * D_MODEL))


def _in_kernel(x_ref, g_ref, sc_ref, sh_ref, w_ref, cs_ref, wg_ref, wg2_ref, bg2_ref,
               o_ref, la_ref, h_scr):
    @pl.when(pl.program_id(1) == 0)
    def _():
        hb = (_rms(x_ref[...], g_ref[...]) * (1.0 + sc_ref[...]) + sh_ref[...]).astype(BF16)
        h_scr[...] = hb
        glr = _dot(hb, wg_ref[...])
        z = _dot(glr.astype(BF16), wg2_ref[...]) + bg2_ref[...]
        la_ref[...] = (jnp.minimum(z, 0.0) - jnp.log(1.0 + jnp.exp(-jnp.abs(z)))) * (1.0 / GATE_TAU)

    o_ref[...] = _dot(h_scr[...], w_ref[...]) * cs_ref[...]


def _in_proj(l, x, norm_g, scale, shift, w_main, col_scale, w_glr, w_g2, b_g2, tm, rows_per_group):
    M = x.shape[0]
    tn = 1024
    R = scale.shape[1]
    grp = lambda i, j: ((i * tm) // rows_per_group, 0, 0)
    return pl.pallas_call(
        _in_kernel,
        grid=(M // tm, MAIN_COLS // tn),
        in_specs=[pl.BlockSpec((tm, D_MODEL), lambda i, j: (i, 0)),
                  pl.BlockSpec((None, 1, D_MODEL), lambda i, j: (l, 0, 0)),
                  pl.BlockSpec((None, R, D_MODEL), grp),
                  pl.BlockSpec((None, R, D_MODEL), grp),
                  pl.BlockSpec((None, D_MODEL, tn), lambda i, j: (l, 0, j)),
                  pl.BlockSpec((1, tn), lambda i, j: (0, j)),
                  pl.BlockSpec((None, D_MODEL, LANES), lambda i, j: (l, 0, 0)),
                  pl.BlockSpec((None, LANES, KEY_B), lambda i, j: (l, 0, 0)),
                  pl.BlockSpec((None, 1, KEY_B), lambda i, j: (l, 0, 0))],
        out_specs=[pl.BlockSpec((tm, tn), lambda i, j: (i, j)),
                   pl.BlockSpec((tm, KEY_B), lambda i, j: (i, 0))],
        out_shape=[jax.ShapeDtypeStruct((M, MAIN_COLS), F32),
                   jax.ShapeDtypeStruct((M, KEY_B), F32)],
        scratch_shapes=[pltpu.VMEM((tm, D_MODEL), BF16)],
        compiler_params=_cparams("parallel", "arbitrary"),
        name="in_proj",
    )(x, norm_g, scale, shift, w_main, col_scale, w_glr, w_g2, b_g2)


def _out_kernel(oa_ref, ob_ref, x_ref, g1_ref, wa_ref, wb_ref, n2_ref, sc_ref, sh_ref, xo_ref, h2_ref):
    y = _dot(oa_ref[...], wa_ref[...]) + _dot(ob_ref[...], wb_ref[...])
    xn = x_ref[...] + g1_ref[...] * y
    xo_ref[...] = xn
    h2_ref[...] = (_rms(xn, n2_ref[...]) * (1.0 + sc_ref[...]) + sh_ref[...]).astype(BF16)


def _out_proj(l, oa, ob, x, gate1, w_out, norm2_g, scale2, shift2, tm, rows_per_group):
    M = x.shape[0]
    R = gate1.shape[1]
    grp = lambda i: ((i * tm) // rows_per_group, 0, 0)
    return pl.pallas_call(
        _out_kernel,
        grid=(M // tm,),
        in_specs=[pl.BlockSpec((tm, MIX_A), lambda i: (i, 0)),
                  pl.BlockSpec((tm, MIX_B), lambda i: (i, 0)),
                  pl.BlockSpec((tm, D_MODEL), lambda i: (i, 0)),
                  pl.BlockSpec((None, R, D_MODEL), grp),
                  pl.BlockSpec((None, MIX_A, D_MODEL), lambda i: (l, 0, 0)),
                  pl.BlockSpec((None, MIX_B, D_MODEL), lambda i: (l, 1, 0)),
                  pl.BlockSpec((None, 1, D_MODEL), lambda i: (l, 0, 0)),
                  pl.BlockSpec((None, R, D_MODEL), grp),
                  pl.BlockSpec((None, R, D_MODEL), grp)],
        out_specs=[pl.BlockSpec((tm, D_MODEL), lambda i: (i, 0)),
                   pl.BlockSpec((tm, D_MODEL), lambda i: (i, 0))],
        out_shape=[jax.ShapeDtypeStruct((M, D_MODEL), F32),
                   jax.ShapeDtypeStruct((M, D_MODEL), BF16)],
        compiler_params=_cparams("parallel"),
        name="out_proj",
    )(oa, ob, x, gate1, w_out, w_out, norm2_g, scale2, shift2)


def _up_kernel(h_ref, wg_ref, wu_ref, cw_ref, cb_ref, hist_ref, act_ref, tail_ref, *, n_seq, steps):
    h = h_ref[...]
    g = _dot(h, wg_ref[...])
    u = _dot(h, wu_ref[...])
    rows = n_seq * steps
    row = lax.broadcasted_iota(jnp.int32, g.shape, 0)

    def per_row(hist):
        if n_seq == 1:
            return jnp.broadcast_to(hist, g.shape)
        return jnp.concatenate([hist] * steps, axis=0)

    h0 = per_row(hist_ref[0])
    h1 = per_row(hist_ref[1])
    g1 = jnp.where(row < n_seq, h1, pltpu.roll(g, n_seq, 0))
    g2 = jnp.where(row < n_seq, h0, jnp.where(row < 2 * n_seq, h1, pltpu.roll(g, 2 * n_seq, 0)))
    cw = cw_ref[...]
    conv = cb_ref[...] + cw[0:1] * g2 + cw[1:2] * g1 + cw[2:3] * g
    act_ref[...] = (_silu(conv) * u).astype(BF16)
    tail = tail_ref.shape[0]
    tail_ref[...] = g[rows - tail:]


def _up_proj(l, h2, w_up, conv_w, conv_b, hist, n_seq, steps, tn):
    M = h2.shape[0]
    tm = n_seq * steps
    nj = D_FF // tn
    tail = max(SUBLANES, (CONV_W - 1) * n_seq)
    return pl.pallas_call(
        functools.partial(_up_kernel, n_seq=n_seq, steps=steps),
        grid=(M // tm, nj),
        in_specs=[pl.BlockSpec((tm, D_MODEL), lambda i, j: (i, 0)),
                  pl.BlockSpec((None, D_MODEL, tn), lambda i, j: (l, 0, j)),
                  pl.BlockSpec((None, D_MODEL, tn), lambda i, j: (l, 0, nj + j)),
                  pl.BlockSpec((None, CONV_W, tn), lambda i, j: (l, 0, j)),
                  pl.BlockSpec((None, 1, tn), lambda i, j: (l, 0, j)),
                  pl.BlockSpec((None, CONV_W - 1, n_seq, tn), lambda i, j: (i, 0, 0, j))],
        out_specs=[pl.BlockSpec((tm, tn), lambda i, j: (i, j)),
                   pl.BlockSpec((None, tail, tn), lambda i, j: (i, 0, j))],
        out_shape=[jax.ShapeDtypeStruct((M, D_FF), BF16),
                   jax.ShapeDtypeStruct((M // tm, tail, D_FF), F32)],
        compiler_params=_cparams("parallel", "parallel"),
        name="up_proj",
    )(h2, w_up, w_up, conv_w, conv_b, hist)


def _down_kernel(a_ref, w_ref, x_ref, g2_ref, o_ref):
    o_ref[...] = x_ref[...] + g2_ref[...] * _dot(a_ref[...], w_ref[...])


def _down_proj(l, act, w_down, x, gate2, tm, tn, rows_per_group):
    M = x.shape[0]
    R = gate2.shape[1]
    return pl.pallas_call(
        _down_kernel,
        grid=(M // tm, D_MODEL // tn),
        in_specs=[pl.BlockSpec((tm, D_FF), lambda i, j: (i, 0)),
                  pl.BlockSpec((None, D_FF, tn), lambda i, j: (l, 0, j)),
                  pl.BlockSpec((tm, tn), lambda i, j: (i, j)),
                  pl.BlockSpec((None, R, tn), lambda i, j: ((i * tm) // rows_per_group, 0, j))],
        out_specs=pl.BlockSpec((tm, tn), lambda i, j: (i, j)),
        out_shape=jax.ShapeDtypeStruct((M, D_MODEL), F32),
        compiler_params=_cparams("parallel", "parallel"),
        name="down_proj",
    )(act, w_down, x, gate2)


def _head_pair_norm(x, g, lo):
    x2 = x * x
    s_lo = jnp.sum(jnp.where(lo, x2, 0.0), axis=-1, keepdims=True)
    s_hi = jnp.sum(jnp.where(lo, 0.0, x2), axis=-1, keepdims=True)
    ms = jnp.where(lo, s_lo, s_hi) * (1.0 / HEAD_DIM)
    return x * lax.rsqrt(ms + EPS) * g


def _attn_p_kernel(q_ref, k_ref, v_ref, bias_ref, qg_ref, kg_ref, og_ref,
                   kn_ref, vo_ref, oa_ref, qs_scr, ks_scr, m_scr, l_scr, acc_scr, *, seq):
    lo = lax.broadcasted_iota(jnp.int32, (1, LANES), 1) < HEAD_DIM
    kn = _head_pair_norm(k_ref[0], kg_ref[...], lo)
    kn_ref[0] = kn
    vo_ref[0] = v_ref[0]
    ks_scr[...] = kn
    qs_scr[...] = _head_pair_norm(q_ref[0], qg_ref[...], lo) * ATTN_SCALE
    m_scr[...] = jnp.full(m_scr.shape, NEG, F32)
    l_scr[...] = jnp.zeros(l_scr.shape, F32)
    acc_scr[...] = jnp.zeros(acc_scr.shape, F32)
    col = lax.broadcasted_iota(jnp.int32, (BAND, 2 * BAND), 1)

    for g, (_, dil) in enumerate(DIL_GROUPS):
        n_blk = seq // dil // BAND

        def rows(start, dil=dil):
            return pl.ds(start, BAND, stride=dil) if dil > 1 else pl.ds(start, BAND)

        def body(idx, carry, g=g, dil=dil, n_blk=n_blk, rows=rows):
            r = idx // n_blk
            nb = idx % n_blk
            sq = r + nb * (BAND * dil)
            sp = jnp.maximum(sq - BAND * dil, r)
            qb = qs_scr[rows(sq), :]
            kcat = jnp.concatenate([ks_scr[rows(sp), :], ks_scr[rows(sq), :]], axis=0).astype(BF16)
            vcat = jnp.concatenate([v_ref[0, rows(sp), :], v_ref[0, rows(sq), :]], axis=0).astype(BF16)
            no_prev = jnp.logical_and(nb == 0, col < BAND)
            parts = []
            for h in range(2):
                hm = lo if h == 0 else jnp.logical_not(lo)
                s = _dot_nt(jnp.where(hm, qb, 0.0).astype(BF16), kcat)
                s = s + jnp.where(no_prev, NEG, bias_ref[g, h])
                mh = jnp.max(s, axis=-1, keepdims=True)
                p = jnp.exp(s - mh)
                lh = jnp.sum(p, axis=-1, keepdims=True)
                parts.append((mh, lh, _dot(p.astype(BF16), vcat)))
            m_blk = jnp.where(lo, parts[0][0], parts[1][0])
            l_blk = jnp.where(lo, parts[0][1], parts[1][1])
            o_blk = jnp.where(lo, parts[0][2], parts[1][2])
            m_old = m_scr[rows(sq), :]
            m_new = jnp.maximum(m_old, m_blk)
            a_old = jnp.exp(m_old - m_new)
            a_blk = jnp.exp(m_blk - m_new)
            l_scr[rows(sq), :] = a_old * l_scr[rows(sq), :] + a_blk * l_blk
            acc_scr[rows(sq), :] = a_old * acc_scr[rows(sq), :] + a_blk * o_blk
            m_scr[rows(sq), :] = m_new
            return carry

        lax.fori_loop(0, seq // BAND, body, 0)

    oa_ref[0] = _head_pair_norm(acc_scr[...] / l_scr[...], og_ref[...], lo).astype(BF16)


def _attn_prompt(proj, bias_tiles, qg, kg, og):
    B, S, _ = proj.shape
    n_pair = MIX_A // LANES
    blk = lambda off: pl.BlockSpec((1, S, LANES), lambda b, hp: (b, 0, off + hp))
    vec = pl.BlockSpec((1, LANES), lambda b, hp: (0, 0))
    out_blk = pl.BlockSpec((1, S, LANES), lambda b, hp: (b, 0, hp))
    return pl.pallas_call(
        functools.partial(_attn_p_kernel, seq=S),
        grid=(B, n_pair),
        in_specs=[blk(0), blk(n_pair), blk(2 * n_pair),
                  pl.BlockSpec((len(DIL_GROUPS), 2, BAND, 2 * BAND), lambda b, hp: (0, hp, 0, 0)),
                  vec, vec, vec],
        out_specs=[out_blk, out_blk, out_blk],
        out_shape=[jax.ShapeDtypeStruct((B, S, MIX_A), F32),
                   jax.ShapeDtypeStruct((B, S, MIX_A), F32),
                   jax.ShapeDtypeStruct((B, S, MIX_A), BF16)],
        scratch_shapes=[pltpu.VMEM((S, LANES), F32)] * 5,
        compiler_params=_cparams("parallel", "parallel"),
        name="attn_prompt",
    )(proj, proj, proj, bias_tiles, qg, kg, og)


def _gla_p_kernel(q_ref, k_ref, v_ref, la_ref, r_ref, g_ref, ob_ref, st_ref, st_scr, *, seq):
    C = GLA_BLOCK
    n_sub = C // GLA_SUB
    st_scr[...] = jnp.zeros(st_scr.shape, F32)
    row = lax.broadcasted_iota(jnp.int32, (C, C), 0)
    tri = row >= lax.broadcasted_iota(jnp.int32, (C, C), 1)
    tri_b = jnp.where(tri, 1.0, 0.0).astype(BF16)
    krow = lax.broadcasted_iota(jnp.int32, (C, DK_B), 0)

    def block(c, carry):
        rs = pl.ds(pl.multiple_of(c * C, C), C)
        la = la_ref[0, rs, :]
        hi = la.astype(BF16)
        r1 = la - hi.astype(F32)
        mid = r1.astype(BF16)
        low = (r1 - mid.astype(F32)).astype(BF16)
        cum = _dot(tri_b, hi) + _dot(tri_b, mid) + _dot(tri_b, low)
        cum_x = cum - la
        q = q_ref[0, rs, :]
        k = k_ref[0, rs, :]
        v = v_ref[0, rs, :]
        vb = v.astype(BF16)
        starts = [cum_x[GLA_SUB * i:GLA_SUB * i + 1, :] for i in range(n_sub)]
        start_rows = jnp.concatenate([jnp.broadcast_to(s, (GLA_SUB, DK_B)) for s in starts], axis=0)
        qe = (q * jnp.exp(cum - start_rows)).astype(BF16)
        atts = []
        for i in range(n_sub):
            expo = jnp.where(krow < GLA_SUB * (i + 1), starts[i] - cum, NEG)
            khat = (k * jnp.exp(expo)).astype(BF16)
            atts.append(_dot_nt(qe[GLA_SUB * i:GLA_SUB * (i + 1)], khat))
        att = jnp.where(tri, jnp.concatenate(atts, axis=0), 0.0).astype(BF16)
        o = _dot(att, vb) + _dot_nt((q * jnp.exp(cum)).astype(BF16), st_scr[...].astype(BF16))
        last = cum[C - 1:C, :]
        kd = (k * jnp.exp(last - cum)).astype(BF16)
        st_scr[...] = st_scr[...] * jnp.exp(last) + _dot(v.T.astype(BF16), kd)
        r = r_ref[0, rs, :]
        ob_ref[0, rs, :] = (_rms(o, g_ref[...]) * _silu(r)).astype(BF16)
        return carry

    lax.fori_loop(0, seq // C, block, 0)
    st_ref[0, 0] = st_scr[...].T


def _gla_prompt(proj, log_a, gla_g):
    B, S, _ = proj.shape
    q0 = 3 * MIX_A // DK_B
    k0 = q0 + H_B
    v0 = (3 * MIX_A + 2 * KEY_B) // DV_B
    r0 = v0 + H_B
    kblk = lambda off: pl.BlockSpec((1, S, DK_B), lambda b, h: (b, 0, off + h))
    vblk = lambda off: pl.BlockSpec((1, S, DV_B), lambda b, h: (b, 0, off + h))
    return pl.pallas_call(
        functools.partial(_gla_p_kernel, seq=S),
        grid=(B, H_B),
        in_specs=[kblk(q0), kblk(k0), vblk(v0), kblk(0), vblk(r0),
                  pl.BlockSpec((1, DV_B), lambda b, h: (0, 0))],
        out_specs=[vblk(0), pl.BlockSpec((1, 1, DK_B, DV_B), lambda b, h: (b, h, 0, 0))],
        out_shape=[jax.ShapeDtypeStruct((B, S, MIX_B), BF16),
                   jax.ShapeDtypeStruct((B, H_B, DK_B, DV_B), F32)],
        scratch_shapes=[pltpu.VMEM((DV_B, DK_B), F32)],
        compiler_params=_cparams("parallel", "parallel"),
        name="gla_prompt",
    )(proj, proj, proj, log_a, proj, gla_g)


def _gla_s_kernel(q_ref, k_ref, v_ref, la_ref, r_ref, g_ref, s_ref, ob_ref, so_ref, *, steps):
    rowi = lax.broadcasted_iota(jnp.int32, (DK_B, DK_B), 0)
    for h in range(H_B):
        ks = slice(DK_B * h, DK_B * (h + 1))
        vs = slice(DV_B * h, DV_B * (h + 1))
        x = jnp.zeros((DK_B, DK_B), F32)
        for t in range(steps):
            x = jnp.where(rowi == t, k_ref[0, t:t + 1, ks], x)
            x = jnp.where(rowi == steps + t, jnp.exp(la_ref[0, t:t + 1, ks]), x)
            x = jnp.where(rowi == 2 * steps + t, q_ref[0, t:t + 1, ks], x)
        xt = x.T
        st = s_ref[0, h]
        for t in range(steps):
            st = xt[:, steps + t:steps + t + 1] * st + xt[:, t:t + 1] * v_ref[0, t:t + 1, vs]
            o = jnp.sum(xt[:, 2 * steps + t:2 * steps + t + 1] * st, axis=0, keepdims=True)
            ob_ref[0, t:t + 1, vs] = _rms(o, g_ref[...]) * _silu(r_ref[0, t:t + 1, vs])
        so_ref[0, h] = st


def _gla_sample(l, q, k, v, log_a, r, gla_g, state):
    B, T, _ = q.shape
    kb = pl.BlockSpec((1, T, KEY_B), lambda b: (b, 0, 0))
    vb = pl.BlockSpec((1, T, MIX_B), lambda b: (b, 0, 0))
    return pl.pallas_call(
        functools.partial(_gla_s_kernel, steps=T),
        grid=(B,),
        in_specs=[kb, kb, vb, kb, vb, pl.BlockSpec((1, DV_B), lambda b: (0, 0)),
                  pl.BlockSpec((None, 1, H_B, DK_B, DV_B), lambda b: (l, b, 0, 0, 0))],
        out_specs=[vb, pl.BlockSpec((1, H_B, DK_B, DV_B), lambda b: (b, 0, 0, 0))],
        out_shape=[jax.ShapeDtypeStruct((B, T, MIX_B), F32),
                   jax.ShapeDtypeStruct((B, H_B, DK_B, DV_B), F32)],
        compiler_params=_cparams("parallel"),
        name="gla_sample",
    )(q, k, v, log_a, r, gla_g, state)


def _attn_s_kernel(q_ref, k_ref, v_ref, ka_ref, va_ref, kb_ref, vb_ref, ba_ref, bb_ref, bn_ref,
                   rep_ref, sel_ref, qg_ref, kg_ref, og_ref, kn_ref, oa_ref, *, n_res):
    kn = _rms(k_ref[0], kg_ref[...])
    kn_ref[0] = kn
    qs = (_rms(q_ref[0], qg_ref[...]) * ATTN_SCALE).astype(BF16)
    n_row = qs.shape[0]
    row_head = lax.broadcasted_iota(jnp.int32, (n_row, MIX_A), 0) % H_A
    col_head = lax.broadcasted_iota(jnp.int32, (n_row, MIX_A), 1) // HEAD_DIM
    hmask = row_head == col_head
    qbd = jnp.where(hmask, _dot(qs, rep_ref[...]), 0.0).astype(BF16)

    s_a = _dot_nt(qbd, ka_ref[...].astype(BF16))
    s_b = jnp.concatenate(
        [_dot_nt(qbd, kb_ref[:, MIX_A * i:MIX_A * (i + 1)].astype(BF16)) for i in range(n_res)], axis=1)
    s_n = _dot_nt(qs, kn.astype(BF16))
    logits_a = [s_a + ba_ref[0], s_a + ba_ref[1]]
    logit_b = s_b + bb_ref[...]
    logits_n = [s_n + bn_ref[i] for i in range(len(DIL_GROUPS))]
    m = jnp.max(logit_b, axis=-1, keepdims=True)
    for x in logits_a + logits_n:
        m = jnp.maximum(m, jnp.max(x, axis=-1, keepdims=True))
    p_a = jnp.exp(logits_a[0] - m) + jnp.exp(logits_a[1] - m)
    p_b = jnp.exp(logit_b - m)
    p_n = jnp.exp(logits_n[0] - m) + jnp.exp(logits_n[1] - m) + jnp.exp(logits_n[2] - m)
    den = (jnp.sum(p_a, axis=-1, keepdims=True) + jnp.sum(p_b, axis=-1, keepdims=True)
           + jnp.sum(p_n, axis=-1, keepdims=True))
    wide = _dot(p_a.astype(BF16), va_ref[...].astype(BF16))
    pb16 = p_b.astype(BF16)
    for i in range(n_res):
        wide = wide + _dot(pb16[:, BAND * i:BAND * (i + 1)], vb_ref[:, MIX_A * i:MIX_A * (i + 1)].astype(BF16))
    wide = jnp.where(hmask, wide, 0.0)
    w_hi = wide.astype(BF16)
    w_lo = (wide - w_hi.astype(F32)).astype(BF16)
    o = _dot(w_hi, sel_ref[...]) + _dot(w_lo, sel_ref[...]) + _dot(p_n.astype(BF16), v_ref[0].astype(BF16))
    oa_ref[0] = _rms(o / den, og_ref[...])


def _attn_sample(l, qkv, cache_k, cache_v, bias_a, bias_b, bias_n, rep, sel, qg, kg, og):
    _, B, n_row, _ = qkv.shape
    steps = n_row // H_A
    W = cache_k.shape[2]
    dil_max = DIL_GROUPS[-1][1]
    n_near = DIL_GROUPS[1][0]
    assert W == DIL_GROUPS[-1][0] and steps <= dil_max and W % n_near == 0
    strided = lambda c: c.reshape(DEPTH, B, W // dil_max, dil_max * MIX_A)
    qblk = lambda i: pl.BlockSpec((None, 1, n_row, HEAD_DIM), lambda b: (i, b, 0, 0))
    near = pl.BlockSpec((None, None, n_near, MIX_A), lambda b: (l, b, W // n_near - 1, 0))
    far = pl.BlockSpec((None, None, W // dil_max, steps * MIX_A), lambda b: (l, b, 0, 0))
    full = lambda a: pl.BlockSpec(a.shape, lambda b: (0,) * a.ndim)
    vec = pl.BlockSpec((1, HEAD_DIM), lambda b: (0, 0))
    out_blk = pl.BlockSpec((1, n_row, HEAD_DIM), lambda b: (b, 0, 0))
    return pl.pallas_call(
        functools.partial(_attn_s_kernel, n_res=steps),
        grid=(B,),
        in_specs=[qblk(0), qblk(1), qblk(2), near, near, far, far,
                  full(bias_a), full(bias_b), full(bias_n), full(rep), full(sel), vec, vec, vec],
        out_specs=[out_blk, out_blk],
        out_shape=[jax.ShapeDtypeStruct((B, n_row, HEAD_DIM), F32),
                   jax.ShapeDtypeStruct((B, n_row, HEAD_DIM), F32)],
        compiler_params=_cparams("parallel"),
        name="attn_sample",
    )(qkv, qkv, qkv, cache_k, cache_v, strided(cache_k), strided(cache_v),
      bias_a, bias_b, bias_n, rep, sel, qg, kg, og)


def _t5_bucket(dist):
    exact = N_BUCKETS // 2
    df = jnp.maximum(dist, 1).astype(F32)
    large = exact + (jnp.log(df / exact) / math.log(MAX_DISTANCE / exact) * (N_BUCKETS - exact)).astype(jnp.int32)
    return jnp.where(dist < exact, dist, jnp.minimum(large, N_BUCKETS - 1))


def _prompt_bias_tiles(rel_bias):
    iq = jnp.arange(BAND)[:, None]
    ik = jnp.arange(2 * BAND)[None, :]
    dist = BAND + iq - ik
    valid = (dist >= 0) & (dist <= BAND)
    tiles = []
    for _, dil in DIL_GROUPS:
        b = jnp.transpose(rel_bias[_t5_bucket(jnp.clip(dist, 0, BAND) * dil)], (2, 0, 1)).astype(F32)
        tiles.append(jnp.where(valid[None], b, NEG))
    return jnp.stack(tiles)


def _sample_bias_tiles(rel_bias, steps, wbuf):
    (w1, d1), (w2, d2), (w3, d3) = DIL_GROUPS
    n_row = steps * H_A
    t = (jnp.arange(n_row) // H_A)[:, None]
    h = (jnp.arange(n_row) % H_A)[:, None]
    look = lambda dist: rel_bias[_t5_bucket(dist), h].astype(F32)
    delta = w2 + t - jnp.arange(w2)[None, :]
    b1 = jnp.where((delta >= 1) & (delta <= w1), look(delta), NEG)
    b2 = jnp.where((delta % d2 == 0) & (delta >= d2) & (delta <= w2), look(delta), NEG)
    res = (jnp.arange(steps * (wbuf // d3)) // (wbuf // d3))[None, :]
    j = (jnp.arange(steps * (wbuf // d3)) % (wbuf // d3))[None, :]
    dist3 = wbuf + t - (res + d3 * j)
    b3 = jnp.where((res == t) & (dist3 >= d3) & (dist3 <= w3), look(dist3), NEG)
    tau = (jnp.arange(n_row) // H_A)[None, :]
    same = (jnp.arange(n_row) % H_A)[None, :] == h
    n1 = jnp.where(same & (tau <= t), look(jnp.maximum(t - tau, 0) * d1), NEG)
    n2 = jnp.where(same & (tau == t), look(jnp.zeros_like(tau) * d2), NEG)
    n3 = jnp.where(same & (tau == t), look(jnp.zeros_like(tau) * d3), NEG)
    return jnp.stack([b1, b2]), b3, jnp.stack([n1, n2, n3])


def kernel(x_prompt, x_sample, cache_k_win, cache_v_win, state_gla, state_conv, c_prompt, c_sample,
           rel_bias, norm1_g, norm2_g, w_ada, b_ada, w_in, q_norm_g, k_norm_g, w_gate2, b_gate2,
           a_out_g, gla_out_g, w_out, w_up, conv_w, conv_b, w_down):
    B, S, _ = x_prompt.shape
    Bs, T, _ = x_sample.shape
    wbuf = cache_k_win.shape[2]
    Mp, Ms = B * S, Bs * T

    w_main = w_in[:, :, :MAIN_COLS].astype(BF16)
    w_glr = jnp.pad(w_in[:, :, MAIN_COLS:], ((0, 0), (0, 0), (0, LANES - GATE_RANK))).astype(BF16)
    w_g2 = jnp.pad(w_gate2, ((0, 0), (0, LANES - GATE_RANK), (0, 0))).astype(BF16)
    b_g2 = b_gate2.reshape(DEPTH, 1, KEY_B)
    w_out_b = w_out.astype(BF16)
    w_up_b = w_up.astype(BF16)
    w_down_b = w_down.astype(BF16)
    col_scale = jnp.ones((1, MAIN_COLS), F32).at[:, 3 * MIX_A:3 * MIX_A + KEY_B].set(DK_B ** -0.5)
    vec3 = lambda a: a.reshape(DEPTH, 1, -1)
    pair = lambda a, l: jnp.tile(a[l], LANES // HEAD_DIM).reshape(1, LANES)

    mods = _ada_mods(jnp.concatenate([c_prompt, c_sample], axis=0), w_ada, b_ada)

    bias_p = _prompt_bias_tiles(rel_bias)
    bias_a, bias_b, bias_n = _sample_bias_tiles(rel_bias, T, wbuf)
    eye = jnp.eye(HEAD_DIM, dtype=BF16)
    rep = jnp.tile(eye, (1, H_A))
    sel = jnp.tile(eye, (H_A, 1))

    cache_k = cache_k_win.reshape(DEPTH, Bs, wbuf, MIX_A)
    cache_v = cache_v_win.reshape(DEPTH, Bs, wbuf, MIX_A)
    hist_p = jnp.zeros((B, CONV_W - 1, 1, D_FF), F32)

    xp = x_prompt.reshape(Mp, D_MODEL)
    xs = jnp.transpose(x_sample, (1, 0, 2)).reshape(Ms, D_MODEL)
    outs = [[] for _ in range(8)]
    for l in range(DEPTH):
        mod_p = [m.reshape(B, 1, D_MODEL) for m in jnp.split(mods[l, :B], 6, axis=-1)]
        mod_s = [jnp.tile(m, (T, 1)).reshape(1, Ms, D_MODEL) for m in jnp.split(mods[l, B:], 6, axis=-1)]

        proj, log_a = _in_proj(l, xp, vec3(norm1_g), mod_p[1], mod_p[0], w_main, col_scale, w_glr, w_g2, b_g2,
                               tm=1024, rows_per_group=S)
        proj = proj.reshape(B, S, MAIN_COLS)
        kn, vv, oa = _attn_prompt(proj, bias_p, pair(q_norm_g, l), pair(k_norm_g, l), pair(a_out_g, l))
        ob, g_st = _gla_prompt(proj, log_a.reshape(B, S, KEY_B), gla_out_g[l].reshape(1, DV_B))
        xp, h2 = _out_proj(l, oa.reshape(Mp, MIX_A), ob.reshape(Mp, MIX_B), xp, mod_p[2], w_out_b,
                           vec3(norm2_g), mod_p[4], mod_p[3], tm=512, rows_per_group=S)
        act, tail = _up_proj(l, h2, w_up_b, conv_w, vec3(conv_b), hist_p, n_seq=1, steps=S, tn=512)
        xp = _down_proj(l, act, w_down_b, xp, mod_p[5], tm=1024, tn=512, rows_per_group=S)
        outs[0].append(kn.reshape(B, S, H_A, HEAD_DIM))
        outs[1].append(vv.reshape(B, S, H_A, HEAD_DIM))
        outs[2].append(g_st)
        outs[3].append(tail[:, tail.shape[1] - (CONV_W - 1):])

        proj, log_a = _in_proj(l, xs, vec3(norm1_g), mod_s[1], mod_s[0], w_main, col_scale, w_glr, w_g2, b_g2,
                               tm=Ms, rows_per_group=Ms)
        bm = lambda a: jnp.transpose(a.reshape(T, Bs, -1), (1, 0, 2))
        qkv = jnp.transpose(proj[:, :3 * MIX_A].reshape(T, Bs, 3, H_A, HEAD_DIM), (2, 1, 0, 3, 4))
        kn_s, oa_s = _attn_sample(l, qkv.reshape(3, Bs, T * H_A, HEAD_DIM), cache_k, cache_v,
                                  bias_a, bias_b, bias_n, rep, sel,
                                  q_norm_g[l].reshape(1, HEAD_DIM), k_norm_g[l].reshape(1, HEAD_DIM),
                                  a_out_g[l].reshape(1, HEAD_DIM))
        gla_in = bm(proj[:, 3 * MIX_A:])
        ob_s, g_st = _gla_sample(l, gla_in[..., :KEY_B], gla_in[..., KEY_B:2 * KEY_B],
                                 gla_in[..., 2 * KEY_B:2 * KEY_B + MIX_B], bm(log_a),
                                 gla_in[..., 2 * KEY_B + MIX_B:], gla_out_g[l].reshape(1, DV_B), state_gla)
        tmaj = lambda a: jnp.transpose(a.reshape(Bs, T, -1), (1, 0, 2)).reshape(Ms, -1).astype(BF16)
        xs, h2 = _out_proj(l, tmaj(oa_s), tmaj(ob_s), xs, mod_s[2], w_out_b,
                           vec3(norm2_g), mod_s[4], mod_s[3], tm=Ms, rows_per_group=Ms)
        hist_s = jnp.transpose(state_conv[l], (1, 0, 2)).reshape(1, CONV_W - 1, Bs, D_FF)
        act, tail = _up_proj(l, h2, w_up_b, conv_w, vec3(conv_b), hist_s, n_seq=Bs, steps=T, tn=512)
        xs = _down_proj(l, act, w_down_b, xs, mod_s[5], tm=Ms, tn=512, rows_per_group=Ms)
        outs[4].append(kn_s.reshape(Bs, T, H_A, HEAD_DIM))
        outs[5].append(jnp.transpose(qkv[2], (0, 1, 2, 3)))
        outs[6].append(g_st)
        outs[7].append(jnp.transpose(tail.reshape(CONV_W - 1, Bs, D_FF), (1, 0, 2)))

    y_prompt = xp.reshape(B, S, D_MODEL)
    y_sample = jnp.transpose(xs.reshape(T, Bs, D_MODEL), (1, 0, 2))
    return (y_prompt, y_sample) + tuple(jnp.stack(o) for o in outs)
```

```python
import functools
import math

import numpy as np
import jax
import jax.numpy as jnp
from jax import lax
from jax.experimental import pallas as pl
from jax.experimental.pallas import tpu as pltpu

F32 = jnp.float32
BF16 = jnp.bfloat16

D_MODEL = 2048
DEPTH = 2
HEAD_DIM = 64
MIX_A = D_MODEL // 2
H_A = MIX_A // HEAD_DIM
DIL_GROUPS = ((128, 1), (512, 4), (2048, 16))
N_BUCKETS = 32
MAX_DISTANCE = 2048
MIX_B = D_MODEL - MIX_A
H_B = 4
DV_B = MIX_B // H_B
DK_B = DV_B // 2
KEY_B = H_B * DK_B
GATE_RANK = 16
GATE_TAU = 16.0
D_FF = ((8 * D_MODEL // 3 + 255) // 256) * 256
CONV_W = 3
EPS = 1e-6
ATTN_SCALE = HEAD_DIM ** -0.5
MAIN_COLS = 3 * MIX_A + 2 * KEY_B + 2 * MIX_B
BAND = 128
NEG = -1e30

LANES = 128
SUBLANES = 8
VMEM_LIMIT = 56 * 1024 * 1024
GLA_BLOCK = 128
GLA_SUB = 16
ATTN_UNROLL = 4

_NT = (((1,), (1,)), ((), ()))


def _cparams(*sem):
    return pltpu.CompilerParams(dimension_semantics=sem, vmem_limit_bytes=VMEM_LIMIT)


def _dot(a, b):
    return jnp.dot(a, b, preferred_element_type=F32)


def _dot_nt(a, b):
    return lax.dot_general(a, b, _NT, preferred_element_type=F32)


def _silu(x):
    return x * jax.nn.sigmoid(x)


def _rms(x, g):
    return x * lax.rsqrt(jnp.mean(x * x, axis=-1, keepdims=True) + EPS) * g


def _ada_kernel(c_ref, w_ref, b_ref, o_ref):
    a = _silu(c_ref[...]).astype(BF16)
    o_ref[...] = _dot(a, w_ref[...].astype(BF16)) + b_ref[...]


def _ada_mods(c_all, w_ada, b_ada):
    G = c_all.shape[0]
    tn = 1024
    return pl.pallas_call(
        _ada_kernel,
        grid=(DEPTH, 6 * D_MODEL // tn),
        in_specs=[pl.BlockSpec((G, D_MODEL), lambda l, j: (0, 0)),
                  pl.BlockSpec((None, D_MODEL, tn), lambda l, j: (l, 0, j)),
                  pl.BlockSpec((None, 1, tn), lambda l, j: (l, 0, j))],
        out_specs=pl.BlockSpec((None, G, tn), lambda l, j: (l, 0, j)),
        out_shape=jax.ShapeDtypeStruct((DEPTH, G, 6 * D_MODEL), F32),
        compiler_params=_cparams("parallel", "parallel"),
        name="ada_mods",
    )(c_all, w_ada, b_ada.reshape(DEPTH, 1, 6 * D_MODEL))


def _in_kernel(x_ref, g_ref, sc_ref, sh_ref, w_ref, cs_ref, wg_ref, wg2_ref, bg2_ref,
               o_ref, la_ref, h_scr):
    @pl.when(pl.program_id(1) == 0)
    def _():
        hb = (_rms(x_ref[...], g_ref[...]) * (1.0 + sc_ref[...]) + sh_ref[...]).astype(BF16)
        h_scr[...] = hb
        glr = _dot(hb, wg_ref[...])
        z = _dot(glr.astype(BF16), wg2_ref[...]) + bg2_ref[...]
        la_ref[...] = (jnp.minimum(z, 0.0) - jnp.log(1.0 + jnp.exp(-jnp.abs(z)))) * (1.0 / GATE_TAU)

    o_ref[...] = _dot(h_scr[...], w_ref[...]) * cs_ref[...]


def _in_proj(l, x, norm_g, scale, shift, w_main, col_scale, w_glr, w_g2, b_g2, tm, rows_per_group):
    M = x.shape[0]
    tn = 1024
    R = scale.shape[1]
    grp = lambda i, j: ((i * tm) // rows_per_group, 0, 0)
    return pl.pallas_call(
        _in_kernel,
        grid=(M // tm, MAIN_COLS // tn),
        in_specs=[pl.BlockSpec((tm, D_MODEL), lambda i, j: (i, 0)),
                  pl.BlockSpec((None, 1, D_MODEL), lambda i, j: (l, 0, 0)),
                  pl.BlockSpec((None, R, D_MODEL), grp),
                  pl.BlockSpec((None, R, D_MODEL), grp),
                  pl.BlockSpec((None, D_MODEL, tn), lambda i, j: (l, 0, j)),
                  pl.BlockSpec((1, tn), lambda i, j: (0, j)),
                  pl.BlockSpec((None, D_MODEL, LANES), lambda i, j: (l, 0, 0)),
                  pl.BlockSpec((None, LANES, KEY_B), lambda i, j: (l, 0, 0)),
                  pl.BlockSpec((None, 1, KEY_B), lambda i, j: (l, 0, 0))],
        out_specs=[pl.BlockSpec((tm, tn), lambda i, j: (i, j)),
                   pl.BlockSpec((tm, KEY_B), lambda i, j: (i, 0))],
        out_shape=[jax.ShapeDtypeStruct((M, MAIN_COLS), F32),
                   jax.ShapeDtypeStruct((M, KEY_B), F32)],
        scratch_shapes=[pltpu.VMEM((tm, D_MODEL), BF16)],
        compiler_params=_cparams("parallel", "arbitrary"),
        name="in_proj",
    )(x, norm_g, scale, shift, w_main, col_scale, w_glr, w_g2, b_g2)


def _out_kernel(oa_ref, ob_ref, x_ref, g1_ref, wa_ref, wb_ref, n2_ref, sc_ref, sh_ref, xo_ref, h2_ref):
    y = _dot(oa_ref[...], wa_ref[...]) + _dot(ob_ref[...], wb_ref[...])
    xn = x_ref[...] + g1_ref[...] * y
    xo_ref[...] = xn
    h2_ref[...] = (_rms(xn, n2_ref[...]) * (1.0 + sc_ref[...]) + sh_ref[...]).astype(BF16)


def _out_proj(l, oa, ob, x, gate1, w_out, norm2_g, scale2, shift2, tm, rows_per_group):
    M = x.shape[0]
    R = gate1.shape[1]
    grp = lambda i: ((i * tm) // rows_per_group, 0, 0)
    return pl.pallas_call(
        _out_kernel,
        grid=(M // tm,),
        in_specs=[pl.BlockSpec((tm, MIX_A), lambda i: (i, 0)),
                  pl.BlockSpec((tm, MIX_B), lambda i: (i, 0)),
                  pl.BlockSpec((tm, D_MODEL), lambda i: (i, 0)),
                  pl.BlockSpec((None, R, D_MODEL), grp),
                  pl.BlockSpec((None, MIX_A, D_MODEL), lambda i: (l, 0, 0)),
                  pl.BlockSpec((None, MIX_B, D_MODEL), lambda i: (l, 1, 0)),
                  pl.BlockSpec((None, 1, D_MODEL), lambda i: (l, 0, 0)),
                  pl.BlockSpec((None, R, D_MODEL), grp),
                  pl.BlockSpec((None, R, D_MODEL), grp)],
        out_specs=[pl.BlockSpec((tm, D_MODEL), lambda i: (i, 0)),
                   pl.BlockSpec((tm, D_MODEL), lambda i: (i, 0))],
        out_shape=[jax.ShapeDtypeStruct((M, D_MODEL), F32),
                   jax.ShapeDtypeStruct((M, D_MODEL), BF16)],
        compiler_params=_cparams("parallel"),
        name="out_proj",
    )(oa, ob, x, gate1, w_out, w_out, norm2_g, scale2, shift2)


def _up_kernel(h_ref, wg_ref, wu_ref, cw_ref, cb_ref, hist_ref, act_ref, tail_ref, *, n_seq, steps):
    h = h_ref[...]
    g = _dot(h, wg_ref[...])
    u = _dot(h, wu_ref[...])
    rows = n_seq * steps
    row = lax.broadcasted_iota(jnp.int32, g.shape, 0)

    def per_row(hist):
        if n_seq == 1:
            return jnp.broadcast_to(hist, g.shape)
        return jnp.concatenate([hist] * steps, axis=0)

    h0 = per_row(hist_ref[0])
    h1 = per_row(hist_ref[1])
    g1 = jnp.where(row < n_seq, h1, pltpu.roll(g, n_seq, 0))
    g2 = jnp.where(row < n_seq, h0, jnp.where(row < 2 * n_seq, h1, pltpu.roll(g, 2 * n_seq, 0)))
    cw = cw_ref[...]
    conv = cb_ref[...] + cw[0:1] * g2 + cw[1:2] * g1 + cw[2:3] * g
    act_ref[...] = (_silu(conv) * u).astype(BF16)
    tail = tail_ref.shape[0]
    tail_ref[...] = g[rows - tail:]


def _up_proj(l, h2, w_up, conv_w, conv_b, hist, n_seq, steps, tn):
    M = h2.shape[0]
    tm = n_seq * steps
    nj = D_FF // tn
    tail = max(SUBLANES, (CONV_W - 1) * n_seq)
    return pl.pallas_call(
        functools.partial(_up_kernel, n_seq=n_seq, steps=steps),
        grid=(M // tm, nj),
        in_specs=[pl.BlockSpec((tm, D_MODEL), lambda i, j: (i, 0)),
                  pl.BlockSpec((None, D_MODEL, tn), lambda i, j: (l, 0, j)),
                  pl.BlockSpec((None, D_MODEL, tn), lambda i, j: (l, 0, nj + j)),
                  pl.BlockSpec((None, CONV_W, tn), lambda i, j: (l, 0, j)),
                  pl.BlockSpec((None, 1, tn), lambda i, j: (l, 0, j)),
                  pl.BlockSpec((None, CONV_W - 1, n_seq, tn), lambda i, j: (i, 0, 0, j))],
        out_specs=[pl.BlockSpec((tm, tn), lambda i, j: (i, j)),
                   pl.BlockSpec((None, tail, tn), lambda i, j: (i, 0, j))],
        out_shape=[jax.ShapeDtypeStruct((M, D_FF), BF16),
                   jax.ShapeDtypeStruct((M // tm, tail, D_FF), F32)],
        compiler_params=_cparams("parallel", "parallel"),
        name="up_proj",
    )(h2, w_up, w_up, conv_w, conv_b, hist)


def _down_kernel(a_ref, w_ref, x_ref, g2_ref, o_ref):
    o_ref[...] = x_ref[...] + g2_ref[...] * _dot(a_ref[...], w_ref[...])


def _down_proj(l, act, w_down, x, gate2, tm, tn, rows_per_group):
    M = x.shape[0]
    R = gate2.shape[1]
    return pl.pallas_call(
        _down_kernel,
        grid=(M // tm, D_MODEL // tn),
        in_specs=[pl.BlockSpec((tm, D_FF), lambda i, j: (i, 0)),
                  pl.BlockSpec((None, D_FF, tn), lambda i, j: (l, 0, j)),
                  pl.BlockSpec((tm, tn), lambda i, j: (i, j)),
                  pl.BlockSpec((None, R, tn), lambda i, j: ((i * tm) // rows_per_group, 0, j))],
        out_specs=pl.BlockSpec((tm, tn), lambda i, j: (i, j)),
        out_shape=jax.ShapeDtypeStruct((M, D_MODEL), F32),
        compiler_params=_cparams("parallel", "parallel"),
        name="down_proj",
    )(act, w_down, x, gate2)


def _head_pair_norm(x, g, lo):
    x2 = x * x
    s_lo = jnp.sum(jnp.where(lo, x2, 0.0), axis=-1, keepdims=True)
    s_hi = jnp.sum(jnp.where(lo, 0.0, x2), axis=-1, keepdims=True)
    ms = jnp.where(lo, s_lo, s_hi) * (1.0 / HEAD_DIM)
    return x * lax.rsqrt(ms + EPS) * g


def _attn_p_kernel(q_ref, k_ref, v_ref, bias_ref, qg_ref, kg_ref, og_ref,
                   knt_ref, vt_ref, oa_ref, qs_scr, ks_scr, m_scr, l_scr, acc_scr, *, seq):
    lo = lax.broadcasted_iota(jnp.int32, (1, LANES), 1) < HEAD_DIM
    kn = _head_pair_norm(k_ref[0], kg_ref[...], lo)
    ks_scr[...] = kn
    knt_ref[0] = kn.T.reshape(2, HEAD_DIM, seq)
    vt_ref[0] = v_ref[0].T.reshape(2, HEAD_DIM, seq)
    qs_scr[...] = _head_pair_norm(q_ref[0], qg_ref[...], lo) * ATTN_SCALE
    col = lax.broadcasted_iota(jnp.int32, (BAND, 2 * BAND), 1)

    for g, (_, dil) in enumerate(DIL_GROUPS):
        n_blk = seq // dil // BAND
        has_prev = n_blk > 1

        def rows(start, dil=dil):
            return pl.ds(start, BAND, stride=dil) if dil > 1 else pl.ds(start, BAND)

        def body(idx, carry, g=g, dil=dil, n_blk=n_blk, rows=rows, has_prev=has_prev):
            r = idx // n_blk
            nb = idx % n_blk
            sq = r + nb * (BAND * dil)
            qb = qs_scr[rows(sq), :]
            kcur = ks_scr[rows(sq), :]
            vcur = v_ref[0, rows(sq), :]
            if has_prev:
                sp = jnp.maximum(sq - BAND * dil, r)
                kcat = jnp.concatenate([ks_scr[rows(sp), :], kcur], axis=0).astype(BF16)
                vcat = jnp.concatenate([v_ref[0, rows(sp), :], vcur], axis=0).astype(BF16)
                no_prev = jnp.logical_and(nb == 0, col < BAND)
            else:
                kcat = kcur.astype(BF16)
                vcat = vcur.astype(BF16)
            parts = []
            for h in range(2):
                hm = lo if h == 0 else jnp.logical_not(lo)
                s = _dot_nt(jnp.where(hm, qb, 0.0).astype(BF16), kcat)
                if has_prev:
                    s = s + jnp.where(no_prev, NEG, bias_ref[g, h])
                else:
                    s = s + bias_ref[g, h, :, BAND:]
                mh = jnp.max(s, axis=-1, keepdims=True)
                p = jnp.exp(s - mh)
                lh = jnp.sum(p, axis=-1, keepdims=True)
                parts.append((mh, lh, _dot(p.astype(BF16), vcat)))
            m_scr[g, rows(sq), :] = jnp.where(lo, parts[0][0], parts[1][0])
            l_scr[g, rows(sq), :] = jnp.where(lo, parts[0][1], parts[1][1])
            acc_scr[g, rows(sq), :] = jnp.where(lo, parts[0][2], parts[1][2])
            return carry

        lax.fori_loop(0, seq // BAND, body, 0, unroll=ATTN_UNROLL)

    n_g = len(DIL_GROUPS)
    m_all = m_scr[0]
    for g in range(1, n_g):
        m_all = jnp.maximum(m_all, m_scr[g])
    num = jnp.zeros((seq, LANES), F32)
    den = jnp.zeros((seq, LANES), F32)
    for g in range(n_g):
        w = jnp.exp(m_scr[g] - m_all)
        num = num + w * acc_scr[g]
        den = den + w * l_scr[g]
    oa_ref[0] = _head_pair_norm(num / den, og_ref[...], lo).astype(BF16)


def _attn_prompt(proj, bias_tiles, qg, kg, og):
    B, S, _ = proj.shape
    n_pair = MIX_A // LANES
    n_g = len(DIL_GROUPS)
    blk = lambda off: pl.BlockSpec((1, S, LANES), lambda b, hp: (b, 0, off + hp))
    vec = pl.BlockSpec((1, LANES), lambda b, hp: (0, 0))
    t_blk = pl.BlockSpec((1, 2, HEAD_DIM, S), lambda b, hp: (b, hp, 0, 0))
    return pl.pallas_call(
        functools.partial(_attn_p_kernel, seq=S),
        grid=(B, n_pair),
        in_specs=[blk(0), blk(n_pair), blk(2 * n_pair),
                  pl.BlockSpec((n_g, 2, BAND, 2 * BAND), lambda b, hp: (0, hp, 0, 0)),
                  vec, vec, vec],
        out_specs=[t_blk, t_blk, pl.BlockSpec((1, S, LANES), lambda b, hp: (b, 0, hp))],
        out_shape=[jax.ShapeDtypeStruct((B, H_A, HEAD_DIM, S), F32),
                   jax.ShapeDtypeStruct((B, H_A, HEAD_DIM, S), F32),
                   jax.ShapeDtypeStruct((B, S, MIX_A), BF16)],
        scratch_shapes=[pltpu.VMEM((S, LANES), F32)] * 2 + [pltpu.VMEM((n_g, S, LANES), F32)] * 3,
        compiler_params=_cparams("parallel", "parallel"),
        name="attn_prompt",
    )(proj, proj, proj, bias_tiles, qg, kg, og)


def _gla_p_kernel(q_ref, k_ref, v_ref, la_ref, r_ref, g_ref, ob_ref, st_ref, st_scr, *, seq):
    C = GLA_BLOCK
    n_sub = C // GLA_SUB
    st_scr[...] = jnp.zeros(st_scr.shape, F32)
    row = lax.broadcasted_iota(jnp.int32, (C, C), 0)
    tri = row >= lax.broadcasted_iota(jnp.int32, (C, C), 1)
    tri_b = jnp.where(tri, 1.0, 0.0).astype(BF16)
    krow = lax.broadcasted_iota(jnp.int32, (C, DK_B), 0)

    def block(c, carry):
        rs = pl.ds(pl.multiple_of(c * C, C), C)
        la = la_ref[0, rs, :]
        hi = la.astype(BF16)
        r1 = la - hi.astype(F32)
        mid = r1.astype(BF16)
        low = (r1 - mid.astype(F32)).astype(BF16)
        cum = _dot(tri_b, hi) + _dot(tri_b, mid) + _dot(tri_b, low)
        cum_x = cum - la
        q = q_ref[0, rs, :]
        k = k_ref[0, rs, :]
        v = v_ref[0, rs, :]
        vb = v.astype(BF16)
        starts = [cum_x[GLA_SUB * i:GLA_SUB * i + 1, :] for i in range(n_sub)]
        start_rows = jnp.concatenate([jnp.broadcast_to(s, (GLA_SUB, DK_B)) for s in starts], axis=0)
        qe = (q * jnp.exp(cum - start_rows)).astype(BF16)
        atts = []
        for i in range(n_sub):
            expo = jnp.where(krow < GLA_SUB * (i + 1), starts[i] - cum, NEG)
            khat = (k * jnp.exp(expo)).astype(BF16)
            atts.append(_dot_nt(qe[GLA_SUB * i:GLA_SUB * (i + 1)], khat))
        att = jnp.where(tri, jnp.concatenate(atts, axis=0), 0.0).astype(BF16)
        o = _dot(att, vb) + _dot_nt((q * jnp.exp(cum)).astype(BF16), st_scr[...].astype(BF16))
        last = cum[C - 1:C, :]
        kd = (k * jnp.exp(last - cum)).astype(BF16)
        st_scr[...] = st_scr[...] * jnp.exp(last) + _dot(v.T.astype(BF16), kd)
        r = r_ref[0, rs, :]
        ob_ref[0, rs, :] = (_rms(o, g_ref[...]) * _silu(r)).astype(BF16)
        return carry

    lax.fori_loop(0, seq // C, block, 0)
    st_ref[0, 0] = st_scr[...].T


def _gla_prompt(proj, log_a, gla_g):
    B, S, _ = proj.shape
    q0 = 3 * MIX_A // DK_B
    k0 = q0 + H_B
    v0 = (3 * MIX_A + 2 * KEY_B) // DV_B
    r0 = v0 + H_B
    kblk = lambda off: pl.BlockSpec((1, S, DK_B), lambda b, h: (b, 0, off + h))
    vblk = lambda off: pl.BlockSpec((1, S, DV_B), lambda b, h: (b, 0, off + h))
    return pl.pallas_call(
        functools.partial(_gla_p_kernel, seq=S),
        grid=(B, H_B),
        in_specs=[kblk(q0), kblk(k0), vblk(v0), kblk(0), vblk(r0),
                  pl.BlockSpec((1, DV_B), lambda b, h: (0, 0))],
        out_specs=[vblk(0), pl.BlockSpec((1, 1, DK_B, DV_B), lambda b, h: (b, h, 0, 0))],
        out_shape=[jax.ShapeDtypeStruct((B, S, MIX_B), BF16),
                   jax.ShapeDtypeStruct((B, H_B, DK_B, DV_B), F32)],
        scratch_shapes=[pltpu.VMEM((DV_B, DK_B), F32)],
        compiler_params=_cparams("parallel", "parallel"),
        name="gla_prompt",
    )(proj, proj, proj, log_a, proj, gla_g)


def _gla_s_kernel(q_ref, k_ref, v_ref, la_ref, r_ref, g_ref, s_ref, ob_ref, so_ref, *, steps):
    rowi = lax.broadcasted_iota(jnp.int32, (DK_B, DK_B), 0)
    for h in range(H_B):
        ks = slice(DK_B * h, DK_B * (h + 1))
        vs = slice(DV_B * h, DV_B * (h + 1))
        x = jnp.zeros((DK_B, DK_B), F32)
        for t in range(steps):
            x = jnp.where(rowi == t, k_ref[0, t:t + 1, ks], x)
            x = jnp.where(rowi == steps + t, jnp.exp(la_ref[0, t:t + 1, ks]), x)
            x = jnp.where(rowi == 2 * steps + t, q_ref[0, t:t + 1, ks], x)
        xt = x.T
        st = s_ref[0, h]
        for t in range(steps):
            st = xt[:, steps + t:steps + t + 1] * st + xt[:, t:t + 1] * v_ref[0, t:t + 1, vs]
            o = jnp.sum(xt[:, 2 * steps + t:2 * steps + t + 1] * st, axis=0, keepdims=True)
            ob_ref[0, t:t + 1, vs] = _rms(o, g_ref[...]) * _silu(r_ref[0, t:t + 1, vs])
        so_ref[0, h] = st


def _gla_sample(l, q, k, v, log_a, r, gla_g, state):
    B, T, _ = q.shape
    kb = pl.BlockSpec((1, T, KEY_B), lambda b: (b, 0, 0))
    vb = pl.BlockSpec((1, T, MIX_B), lambda b: (b, 0, 0))
    return pl.pallas_call(
        functools.partial(_gla_s_kernel, steps=T),
        grid=(B,),
        in_specs=[kb, kb, vb, kb, vb, pl.BlockSpec((1, DV_B), lambda b: (0, 0)),
                  pl.BlockSpec((None, 1, H_B, DK_B, DV_B), lambda b: (l, b, 0, 0, 0))],
        out_specs=[vb, pl.BlockSpec((1, H_B, DK_B, DV_B), lambda b: (b, 0, 0, 0))],
        out_shape=[jax.ShapeDtypeStruct((B, T, MIX_B), F32),
                   jax.ShapeDtypeStruct((B, H_B, DK_B, DV_B), F32)],
        compiler_params=_cparams("parallel"),
        name="gla_sample",
    )(q, k, v, log_a, r, gla_g, state)


def _attn_s_kernel(q_ref, k_ref, v_ref, kt_ref, vt_ref, bnear_ref, bfar_ref, bnew_ref,
                   qg_ref, kg_ref, og_ref, kn_ref, oa_ref, kpad_scr, vpad_scr, *, n_near):
    n_head, n_row, _ = kn_ref.shape[1:]
    width = kt_ref.shape[-1]
    kn = _rms(k_ref[0], kg_ref[...])
    kn_ref[0] = kn
    qs = _rms(q_ref[0], qg_ref[...]) * ATTN_SCALE
    kpad_scr[...] = jnp.zeros(kpad_scr.shape, F32)
    vpad_scr[...] = jnp.zeros(vpad_scr.shape, F32)
    kpad_scr[:, 0:n_row, :] = kn
    vpad_scr[:, 0:n_row, :] = v_ref[0]
    n_g = len(DIL_GROUPS)
    for h in range(n_head):
        qh = qs[h].astype(BF16)
        s = _dot(qh, kt_ref[h].astype(BF16))
        s_new = _dot_nt(qh, kpad_scr[h].astype(BF16))
        far = s + bfar_ref[h]
        s_near = s[:, width - n_near:]
        near = [s_near + bnear_ref[g, h] for g in range(n_g - 1)]
        new = [s_new + bnew_ref[g, h] for g in range(n_g)]
        m = jnp.max(far, axis=-1, keepdims=True)
        for x in near + new:
            m = jnp.maximum(m, jnp.max(x, axis=-1, keepdims=True))
        p_far = jnp.exp(far - m)
        p_near = sum(jnp.exp(x - m) for x in near)
        p_new = sum(jnp.exp(x - m) for x in new)
        den = (jnp.sum(p_far, axis=-1, keepdims=True) + jnp.sum(p_near, axis=-1, keepdims=True)
               + jnp.sum(p_new, axis=-1, keepdims=True))
        p = jnp.concatenate([p_far[:, :width - n_near], p_far[:, width - n_near:] + p_near], axis=1)
        o = _dot_nt(p.astype(BF16), vt_ref[h].astype(BF16)) + _dot(p_new.astype(BF16), vpad_scr[h].astype(BF16))
        oa_ref[0, h] = _rms(o / den, og_ref[...])


def _attn_sample(l, qkv, cache_kt, cache_vt, bias_near, bias_far, bias_new, qg, kg, og):
    _, B, n_head, n_row, _ = qkv.shape
    W = cache_kt.shape[-1]
    n_near = bias_near.shape[-1]
    qblk = lambda i: pl.BlockSpec((None, 1, n_head, n_row, HEAD_DIM), lambda b: (i, b, 0, 0, 0))
    cblk = pl.BlockSpec((None, None, n_head, HEAD_DIM, W), lambda b: (l, b, 0, 0, 0))
    full = lambda a: pl.BlockSpec(a.shape, lambda b: (0,) * a.ndim)
    vec = pl.BlockSpec((1, HEAD_DIM), lambda b: (0, 0))
    out_blk = pl.BlockSpec((1, n_head, n_row, HEAD_DIM), lambda b: (b, 0, 0, 0))
    return pl.pallas_call(
        functools.partial(_attn_s_kernel, n_near=n_near),
        grid=(B,),
        in_specs=[qblk(0), qblk(1), qblk(2), cblk, cblk,
                  full(bias_near), full(bias_far), full(bias_new), vec, vec, vec],
        out_specs=[out_blk, out_blk],
        out_shape=[jax.ShapeDtypeStruct((B, n_head, n_row, HEAD_DIM), F32),
                   jax.ShapeDtypeStruct((B, n_head, n_row, HEAD_DIM), F32)],
        scratch_shapes=[pltpu.VMEM((n_head, BAND, HEAD_DIM), F32)] * 2,
        compiler_params=_cparams("parallel"),
        name="attn_sample",
    )(qkv, qkv, qkv, cache_kt, cache_vt, bias_near, bias_far, bias_new, qg, kg, og)


def _bucket_table(n):
    dist = np.arange(n)
    exact = N_BUCKETS // 2
    df = np.maximum(dist, 1).astype(np.float32)
    large = exact + (np.log(df / np.float32(exact)) / np.float32(math.log(MAX_DISTANCE / exact))
                     * np.float32(N_BUCKETS - exact)).astype(np.int32)
    return np.where(dist < exact, dist, np.minimum(large, N_BUCKETS - 1))


def _prompt_bias_tiles(rel_bias):
    span = 3 * BAND
    tiles = []
    for _, dil in DIL_GROUPS:
        steps = np.arange(BAND, -1, -1)
        vec = rel_bias[_bucket_table(BAND * dil + 1)[steps * dil]].T.astype(F32)
        u = jnp.concatenate([vec, jnp.full((H_A, span - BAND - 1), NEG, F32)], axis=1)
        t = jnp.tile(u, (1, BAND))[:, :BAND * (span - 1)].reshape(H_A, BAND, span - 1)
        tiles.append(t[:, :, :2 * BAND])
    return jnp.stack(tiles)


def _sample_bias_tiles(rel_bias, steps, rows, wbuf):
    n_dist = wbuf + rows
    bucket = _bucket_table(n_dist)
    dist = np.arange(n_dist)
    tab = rel_bias[bucket].T.astype(F32)
    n_near = DIL_GROUPS[-2][0]
    tiles = []
    for win, dil in DIL_GROUPS:
        ok = (dist % dil == 0) & (dist >= dil) & (dist <= win)
        rev = jnp.where(ok[None, ::-1], tab[:, ::-1], NEG)
        tiles.append(jnp.stack([rev[:, rows - 1 - t:rows - 1 - t + wbuf] for t in range(rows)], axis=1))
    near = jnp.stack([t[:, :, wbuf - n_near:] for t in tiles[:-1]])
    far = tiles[-1]
    t_new = np.arange(rows)[:, None]
    tau = np.arange(BAND)[None, :]
    live = (t_new < steps) & (tau < steps)
    back = np.clip(t_new - tau, 0, steps)
    new = []
    for g, (_, dil) in enumerate(DIL_GROUPS):
        ok = live & ((tau <= t_new) if g == 0 else (tau == t_new))
        new.append(jnp.where(ok[None], rel_bias[bucket[back * dil]].transpose(2, 0, 1).astype(F32), NEG))
    return near, far, jnp.stack(new)


def kernel(x_prompt, x_sample, cache_k_win, cache_v_win, state_gla, state_conv, c_prompt, c_sample,
           rel_bias, norm1_g, norm2_g, w_ada, b_ada, w_in, q_norm_g, k_norm_g, w_gate2, b_gate2,
           a_out_g, gla_out_g, w_out, w_up, conv_w, conv_b, w_down):
    B, S, _ = x_prompt.shape
    Bs, T, _ = x_sample.shape
    wbuf = cache_k_win.shape[2]
    Mp, Ms = B * S, Bs * T
    assert wbuf == DIL_GROUPS[-1][0] and T <= SUBLANES

    w_main = w_in[:, :, :MAIN_COLS].astype(BF16)
    w_glr = jnp.pad(w_in[:, :, MAIN_COLS:], ((0, 0), (0, 0), (0, LANES - GATE_RANK))).astype(BF16)
    w_g2 = jnp.pad(w_gate2, ((0, 0), (0, LANES - GATE_RANK), (0, 0))).astype(BF16)
    b_g2 = b_gate2.reshape(DEPTH, 1, KEY_B)
    w_out_b = w_out.astype(BF16)
    w_up_b = w_up.astype(BF16)
    w_down_b = w_down.astype(BF16)
    col_scale = jnp.ones((1, MAIN_COLS), F32).at[:, 3 * MIX_A:3 * MIX_A + KEY_B].set(DK_B ** -0.5)
    vec3 = lambda a: a.reshape(DEPTH, 1, -1)
    pair = lambda a, l: jnp.tile(a[l], LANES // HEAD_DIM).reshape(1, LANES)
    head = lambda a, l: a[l].reshape(1, HEAD_DIM)

    mods = _ada_mods(jnp.concatenate([c_prompt, c_sample], axis=0), w_ada, b_ada)

    bias_p = _prompt_bias_tiles(rel_bias)
    bias_near, bias_far, bias_new = _sample_bias_tiles(rel_bias, T, SUBLANES, wbuf)

    cache_kt = jnp.transpose(cache_k_win, (0, 1, 3, 4, 2))
    cache_vt = jnp.transpose(cache_v_win, (0, 1, 3, 4, 2))
    hist_p = jnp.zeros((B, CONV_W - 1, 1, D_FF), F32)

    xp = x_prompt.reshape(Mp, D_MODEL)
    xs = jnp.transpose(x_sample, (1, 0, 2)).reshape(Ms, D_MODEL)
    outs = [[] for _ in range(8)]
    for l in range(DEPTH):
        mod_p = [m.reshape(B, 1, D_MODEL) for m in jnp.split(mods[l, :B], 6, axis=-1)]
        mod_s = [jnp.tile(m, (T, 1)).reshape(1, Ms, D_MODEL) for m in jnp.split(mods[l, B:], 6, axis=-1)]

        proj, log_a = _in_proj(l, xp, vec3(norm1_g), mod_p[1], mod_p[0], w_main, col_scale, w_glr, w_g2, b_g2,
                               tm=1024, rows_per_group=S)
        proj = proj.reshape(B, S, MAIN_COLS)
        knt, vt, oa = _attn_prompt(proj, bias_p, pair(q_norm_g, l), pair(k_norm_g, l), pair(a_out_g, l))
        ob, g_st = _gla_prompt(proj, log_a.reshape(B, S, KEY_B), gla_out_g[l].reshape(1, DV_B))
        xp, h2 = _out_proj(l, oa.reshape(Mp, MIX_A), ob.reshape(Mp, MIX_B), xp, mod_p[2], w_out_b,
                           vec3(norm2_g), mod_p[4], mod_p[3], tm=512, rows_per_group=S)
        act, tail = _up_proj(l, h2, w_up_b, conv_w, vec3(conv_b), hist_p, n_seq=1, steps=S, tn=512)
        xp = _down_proj(l, act, w_down_b, xp, mod_p[5], tm=1024, tn=512, rows_per_group=S)
        outs[0].append(knt)
        outs[1].append(vt)
        outs[2].append(g_st)
        outs[3].append(tail[:, tail.shape[1] - (CONV_W - 1):])

        proj, log_a = _in_proj(l, xs, vec3(norm1_g), mod_s[1], mod_s[0], w_main, col_scale, w_glr, w_g2, b_g2,
                               tm=Ms, rows_per_group=Ms)
        bm = lambda a: jnp.transpose(a.reshape(T, Bs, -1), (1, 0, 2))
        qkv = jnp.transpose(proj[:, :3 * MIX_A].reshape(T, Bs, 3, H_A, HEAD_DIM), (2, 1, 3, 0, 4))
        qkv_pad = jnp.pad(qkv, ((0, 0), (0, 0), (0, 0), (0, SUBLANES - T), (0, 0)))
        kn_s, oa_s = _attn_sample(l, qkv_pad, cache_kt, cache_vt, bias_near, bias_far, bias_new,
                                  head(q_norm_g, l), head(k_norm_g, l), head(a_out_g, l))
        gla_in = bm(proj[:, 3 * MIX_A:])
        ob_s, g_st = _gla_sample(l, gla_in[..., :KEY_B], gla_in[..., KEY_B:2 * KEY_B],
                                 gla_in[..., 2 * KEY_B:2 * KEY_B + MIX_B], bm(log_a),
                                 gla_in[..., 2 * KEY_B + MIX_B:], gla_out_g[l].reshape(1, DV_B), state_gla)
        oa_t = jnp.transpose(oa_s[:, :, :T], (2, 0, 1, 3)).reshape(Ms, MIX_A).astype(BF16)
        ob_t = jnp.transpose(ob_s, (1, 0, 2)).reshape(Ms, MIX_B).astype(BF16)
        xs, h2 = _out_proj(l, oa_t, ob_t, xs, mod_s[2], w_out_b,
                           vec3(norm2_g), mod_s[4], mod_s[3], tm=Ms, rows_per_group=Ms)
        hist_s = jnp.transpose(state_conv[l], (1, 0, 2)).reshape(1, CONV_W - 1, Bs, D_FF)
        act, tail = _up_proj(l, h2, w_up_b, conv_w, vec3(conv_b), hist_s, n_seq=Bs, steps=T, tn=512)
        xs = _down_proj(l, act, w_down_b, xs, mod_s[5], tm=Ms, tn=512, rows_per_group=Ms)
        outs[4].append(jnp.transpose(kn_s[:, :, :T], (0, 2, 1, 3)))
        outs[5].append(jnp.transpose(qkv[2], (0, 2, 1, 3)))
        outs[6].append(g_st)
        outs[7].append(jnp.transpose(tail.reshape(CONV_W - 1, Bs, D_FF), (1, 0, 2)))

    y_prompt = xp.reshape(B, S, D_MODEL)
    y_sample = jnp.transpose(xs.reshape(T, Bs, D_MODEL), (1, 0, 2))
    stacked = [jnp.stack(o) for o in outs]
    stacked[0] = jnp.transpose(stacked[0], (0, 1, 4, 2, 3))
    stacked[1] = jnp.transpose(stacked[1], (0, 1, 4, 2, 3))
    return (y_prompt, y_sample) + tuple(stacked)
```

```python
import functools
import math

import numpy as np
import jax
import jax.numpy as jnp
from jax import lax
from jax.experimental import pallas as pl
from jax.experimental.pallas import tpu as pltpu

F32 = jnp.float32
BF16 = jnp.bfloat16

D_MODEL = 2048
DEPTH = 2
HEAD_DIM = 64
MIX_A = D_MODEL // 2
H_A = MIX_A // HEAD_DIM
DIL_GROUPS = ((128, 1), (512, 4), (2048, 16))
N_BUCKETS = 32
MAX_DISTANCE = 2048
MIX_B = D_MODEL - MIX_A
H_B = 4
DV_B = MIX_B // H_B
DK_B = DV_B // 2
KEY_B = H_B * DK_B
GATE_RANK = 16
GATE_TAU = 16.0
D_FF = ((8 * D_MODEL // 3 + 255) // 256) * 256
CONV_W = 3
EPS = 1e-6
ATTN_SCALE = HEAD_DIM ** -0.5
MAIN_COLS = 3 * MIX_A + 2 * KEY_B + 2 * MIX_B
BAND = 128
NEG = -1e30

LANES = 128
SUBLANES = 8
VMEM_LIMIT = 56 * 1024 * 1024
TM_IN, TM_OUT, TM_DOWN = 1024, 512, 1024
TN_UP, TN_DOWN = 512, 512
GLA_BLOCK = 128
GLA_SUB = 16
ATTN_PAR = 8
GLA_UNROLL = 2
GLA_PAR = 2

_NT = (((1,), (1,)), ((), ()))


def _cparams(*sem):
    return pltpu.CompilerParams(dimension_semantics=sem, vmem_limit_bytes=VMEM_LIMIT)


def _dot(a, b):
    return jnp.dot(a, b, preferred_element_type=F32)


def _dot_nt(a, b):
    return lax.dot_general(a, b, _NT, preferred_element_type=F32)


def _silu(x):
    return x * jax.nn.sigmoid(x)


def _rms(x, g):
    return x * lax.rsqrt(jnp.mean(x * x, axis=-1, keepdims=True) + EPS) * g


def _ada_kernel(c_ref, w_ref, b_ref, o_ref):
    a = _silu(c_ref[...]).astype(BF16)
    o_ref[...] = _dot(a, w_ref[...].astype(BF16)) + b_ref[...]


def _ada_mods(c_all, w_ada, b_ada):
    G = c_all.shape[0]
    tn = 1024
    return pl.pallas_call(
        _ada_kernel,
        grid=(DEPTH, 6 * D_MODEL // tn),
        in_specs=[pl.BlockSpec((G, D_MODEL), lambda l, j: (0, 0)),
                  pl.BlockSpec((None, D_MODEL, tn), lambda l, j: (l, 0, j)),
                  pl.BlockSpec((None, 1, tn), lambda l, j: (l, 0, j))],
        out_specs=pl.BlockSpec((None, G, tn), lambda l, j: (l, 0, j)),
        out_shape=jax.ShapeDtypeStruct((DEPTH, G, 6 * D_MODEL), F32),
        compiler_params=_cparams("parallel", "parallel"),
        name="ada_mods",
    )(c_all, w_ada, b_ada.reshape(DEPTH, 1, 6 * D_MODEL))


def _in_kernel(x_ref, g_ref, sc_ref, sh_ref, w_ref, cs_ref, wg_ref, wg2_ref, bg2_ref,
               o_ref, la_ref, h_scr):
    @pl.when(pl.program_id(1) == 0)
    def _():
        hb = (_rms(x_ref[...], g_ref[...]) * (1.0 + sc_ref[...]) + sh_ref[...]).astype(BF16)
        h_scr[...] = hb
        glr = _dot(hb, wg_ref[...])
        z = _dot(glr.astype(BF16), wg2_ref[...]) + bg2_ref[...]
        la_ref[...] = (jnp.minimum(z, 0.0) - jnp.log(1.0 + jnp.exp(-jnp.abs(z)))) * (1.0 / GATE_TAU)

    o_ref[...] = _dot(h_scr[...], w_ref[...]) * cs_ref[...]


def _in_proj(l, x, norm_g, scale, shift, w_main, col_scale, w_glr, w_g2, b_g2, tm, rows_per_group):
    M = x.shape[0]
    tn = 1024
    R = scale.shape[1]
    grp = lambda i, j: ((i * tm) // rows_per_group, 0, 0)
    return pl.pallas_call(
        _in_kernel,
        grid=(M // tm, MAIN_COLS // tn),
        in_specs=[pl.BlockSpec((tm, D_MODEL), lambda i, j: (i, 0)),
                  pl.BlockSpec((None, 1, D_MODEL), lambda i, j: (l, 0, 0)),
                  pl.BlockSpec((None, R, D_MODEL), grp),
                  pl.BlockSpec((None, R, D_MODEL), grp),
                  pl.BlockSpec((None, D_MODEL, tn), lambda i, j: (l, 0, j)),
                  pl.BlockSpec((1, tn), lambda i, j: (0, j)),
                  pl.BlockSpec((None, D_MODEL, LANES), lambda i, j: (l, 0, 0)),
                  pl.BlockSpec((None, LANES, KEY_B), lambda i, j: (l, 0, 0)),
                  pl.BlockSpec((None, 1, KEY_B), lambda i, j: (l, 0, 0))],
        out_specs=[pl.BlockSpec((tm, tn), lambda i, j: (i, j)),
                   pl.BlockSpec((tm, KEY_B), lambda i, j: (i, 0))],
        out_shape=[jax.ShapeDtypeStruct((M, MAIN_COLS), F32),
                   jax.ShapeDtypeStruct((M, KEY_B), F32)],
        scratch_shapes=[pltpu.VMEM((tm, D_MODEL), BF16)],
        compiler_params=_cparams("parallel", "arbitrary"),
        name="in_proj",
    )(x, norm_g, scale, shift, w_main, col_scale, w_glr, w_g2, b_g2)


def _out_kernel(oa_ref, ob_ref, x_ref, g1_ref, wa_ref, wb_ref, n2_ref, sc_ref, sh_ref, xo_ref, h2_ref):
    y = _dot(oa_ref[...], wa_ref[...]) + _dot(ob_ref[...], wb_ref[...])
    xn = x_ref[...] + g1_ref[...] * y
    xo_ref[...] = xn
    h2_ref[...] = (_rms(xn, n2_ref[...]) * (1.0 + sc_ref[...]) + sh_ref[...]).astype(BF16)


def _out_proj(l, oa, ob, x, gate1, w_out, norm2_g, scale2, shift2, tm, rows_per_group):
    M = x.shape[0]
    R = gate1.shape[1]
    grp = lambda i: ((i * tm) // rows_per_group, 0, 0)
    return pl.pallas_call(
        _out_kernel,
        grid=(M // tm,),
        in_specs=[pl.BlockSpec((tm, MIX_A), lambda i: (i, 0)),
                  pl.BlockSpec((tm, MIX_B), lambda i: (i, 0)),
                  pl.BlockSpec((tm, D_MODEL), lambda i: (i, 0)),
                  pl.BlockSpec((None, R, D_MODEL), grp),
                  pl.BlockSpec((None, MIX_A, D_MODEL), lambda i: (l, 0, 0)),
                  pl.BlockSpec((None, MIX_B, D_MODEL), lambda i: (l, 1, 0)),
                  pl.BlockSpec((None, 1, D_MODEL), lambda i: (l, 0, 0)),
                  pl.BlockSpec((None, R, D_MODEL), grp),
                  pl.BlockSpec((None, R, D_MODEL), grp)],
        out_specs=[pl.BlockSpec((tm, D_MODEL), lambda i: (i, 0)),
                   pl.BlockSpec((tm, D_MODEL), lambda i: (i, 0))],
        out_shape=[jax.ShapeDtypeStruct((M, D_MODEL), F32),
                   jax.ShapeDtypeStruct((M, D_MODEL), BF16)],
        compiler_params=_cparams("parallel"),
        name="out_proj",
    )(oa, ob, x, gate1, w_out, w_out, norm2_g, scale2, shift2)


def _up_kernel(h_ref, wg_ref, wu_ref, cw_ref, cb_ref, hist_ref, act_ref, tail_ref, *, n_seq, steps):
    h = h_ref[...]
    g = _dot(h, wg_ref[...])
    u = _dot(h, wu_ref[...])
    rows = n_seq * steps
    row = lax.broadcasted_iota(jnp.int32, g.shape, 0)

    def per_row(hist):
        if n_seq == 1:
            return jnp.broadcast_to(hist, g.shape)
        return jnp.concatenate([hist] * steps, axis=0)

    h0 = per_row(hist_ref[0])
    h1 = per_row(hist_ref[1])
    g1 = jnp.where(row < n_seq, h1, pltpu.roll(g, n_seq, 0))
    g2 = jnp.where(row < n_seq, h0, jnp.where(row < 2 * n_seq, h1, pltpu.roll(g, 2 * n_seq, 0)))
    cw = cw_ref[...]
    conv = cb_ref[...] + cw[0:1] * g2 + cw[1:2] * g1 + cw[2:3] * g
    act_ref[...] = (_silu(conv) * u).astype(BF16)
    tail = tail_ref.shape[0]
    tail_ref[...] = g[rows - tail:]


def _up_proj(l, h2, w_up, conv_w, conv_b, hist, n_seq, steps, tn):
    M = h2.shape[0]
    tm = n_seq * steps
    nj = D_FF // tn
    tail = max(SUBLANES, (CONV_W - 1) * n_seq)
    return pl.pallas_call(
        functools.partial(_up_kernel, n_seq=n_seq, steps=steps),
        grid=(M // tm, nj),
        in_specs=[pl.BlockSpec((tm, D_MODEL), lambda i, j: (i, 0)),
                  pl.BlockSpec((None, D_MODEL, tn), lambda i, j: (l, 0, j)),
                  pl.BlockSpec((None, D_MODEL, tn), lambda i, j: (l, 0, nj + j)),
                  pl.BlockSpec((None, CONV_W, tn), lambda i, j: (l, 0, j)),
                  pl.BlockSpec((None, 1, tn), lambda i, j: (l, 0, j)),
                  pl.BlockSpec((None, CONV_W - 1, n_seq, tn), lambda i, j: (i, 0, 0, j))],
        out_specs=[pl.BlockSpec((tm, tn), lambda i, j: (i, j)),
                   pl.BlockSpec((None, tail, tn), lambda i, j: (i, 0, j))],
        out_shape=[jax.ShapeDtypeStruct((M, D_FF), BF16),
                   jax.ShapeDtypeStruct((M // tm, tail, D_FF), F32)],
        compiler_params=_cparams("parallel", "parallel"),
        name="up_proj",
    )(h2, w_up, w_up, conv_w, conv_b, hist)


def _down_kernel(a_ref, w_ref, x_ref, g2_ref, o_ref):
    o_ref[...] = x_ref[...] + g2_ref[...] * _dot(a_ref[...], w_ref[...])


def _down_proj(l, act, w_down, x, gate2, tm, tn, rows_per_group):
    M = x.shape[0]
    R = gate2.shape[1]
    return pl.pallas_call(
        _down_kernel,
        grid=(M // tm, D_MODEL // tn),
        in_specs=[pl.BlockSpec((tm, D_FF), lambda i, j: (i, 0)),
                  pl.BlockSpec((None, D_FF, tn), lambda i, j: (l, 0, j)),
                  pl.BlockSpec((tm, tn), lambda i, j: (i, j)),
                  pl.BlockSpec((None, R, tn), lambda i, j: ((i * tm) // rows_per_group, 0, j))],
        out_specs=pl.BlockSpec((tm, tn), lambda i, j: (i, j)),
        out_shape=jax.ShapeDtypeStruct((M, D_MODEL), F32),
        compiler_params=_cparams("parallel", "parallel"),
        name="down_proj",
    )(act, w_down, x, gate2)


def _head_pair_norm(x, g, lo):
    x2 = x * x
    s_lo = jnp.sum(jnp.where(lo, x2, 0.0), axis=-1, keepdims=True)
    s_hi = jnp.sum(jnp.where(lo, 0.0, x2), axis=-1, keepdims=True)
    ms = jnp.where(lo, s_lo, s_hi) * (1.0 / HEAD_DIM)
    return x * lax.rsqrt(ms + EPS) * g


def _attn_p_kernel(q_ref, k_ref, v_ref, bias_ref, qg_ref, kg_ref, og_ref,
                   knt_ref, vt_ref, oa_ref, qs_scr, ks_scr, perm_scr, m_scr, l_scr, acc_scr, *, seq):
    lo = lax.broadcasted_iota(jnp.int32, (1, LANES), 1) < HEAD_DIM
    kn = _head_pair_norm(k_ref[0], kg_ref[...], lo)
    ks_scr[...] = kn
    knt_ref[0] = kn.T.reshape(2, HEAD_DIM, seq)
    vt_ref[0] = v_ref[0].T.reshape(2, HEAD_DIM, seq)
    qs_scr[...] = _head_pair_norm(q_ref[0], qg_ref[...], lo) * ATTN_SCALE
    col = lax.broadcasted_iota(jnp.int32, (BAND, 2 * BAND), 1)

    stage_rows = [(lambda rs: qs_scr[rs, :], lambda rs: ks_scr[rs, :], lambda rs: v_ref[0, rs, :])]
    for g in range(1, len(DIL_GROUPS)):
        d_prev, d_cur = DIL_GROUPS[g - 1][1], DIL_GROUPS[g][1]
        ratio = d_cur // d_prev
        len_prev, len_cur = seq // d_prev, seq // d_cur
        for t in range(3):
            for c in range(d_prev):
                for m in range(ratio):
                    src = pl.ds(c * len_prev + m, len_cur, stride=ratio)
                    perm_scr[g - 1, t, pl.ds((c + d_prev * m) * len_cur, len_cur), :] = stage_rows[g - 1][t](src)
        stage_rows.append(tuple((lambda rs, g=g, t=t: perm_scr[g - 1, t, rs, :]) for t in range(3)))

    for g, (_, dil) in enumerate(DIL_GROUPS):
        n_blk = seq // dil // BAND
        has_prev = n_blk > 1
        q_rows, k_rows, v_rows = stage_rows[g]

        def rows(start, dil=dil):
            return pl.ds(start, BAND, stride=dil) if dil > 1 else pl.ds(start, BAND)

        def body(trip, carry, g=g, dil=dil, n_blk=n_blk, rows=rows, has_prev=has_prev,
                 q_rows=q_rows, k_rows=k_rows, v_rows=v_rows):
            blocks = []
            for u in range(ATTN_PAR):
                idx = trip * ATTN_PAR + u
                r = idx // n_blk
                nb = idx % n_blk
                cur = pl.ds(pl.multiple_of((r * n_blk + nb) * BAND, BAND), BAND)
                qb = q_rows(cur)
                kcat = k_rows(cur)
                vcat = v_rows(cur)
                bias = [bias_ref[g, h, :, BAND:] for h in range(2)]
                if has_prev:
                    prev = pl.ds(pl.multiple_of((r * n_blk + jnp.maximum(nb - 1, 0)) * BAND, BAND), BAND)
                    kcat = jnp.concatenate([k_rows(prev), kcat], axis=0)
                    vcat = jnp.concatenate([v_rows(prev), vcat], axis=0)
                    no_prev = jnp.logical_and(nb == 0, col < BAND)
                    bias = [jnp.where(no_prev, NEG, bias_ref[g, h]) for h in range(2)]
                blocks.append(dict(out=rows(r + nb * (BAND * dil)), q=qb, k=kcat.astype(BF16),
                                   v=vcat.astype(BF16), bias=bias))
            heads = [(blk, h) for blk in blocks for h in range(2)]
            s = [_dot_nt(jnp.where(lo if h == 0 else jnp.logical_not(lo), blk["q"], 0.0).astype(BF16), blk["k"])
                 + blk["bias"][h] for blk, h in heads]
            mx = [jnp.max(x, axis=-1, keepdims=True) for x in s]
            p = [jnp.exp(x - m) for x, m in zip(s, mx)]
            sm = [jnp.sum(x, axis=-1, keepdims=True) for x in p]
            o = [_dot(x.astype(BF16), blk["v"]) for x, (blk, _) in zip(p, heads)]
            for u, blk in enumerate(blocks):
                m_scr[g, blk["out"], :] = jnp.where(lo, mx[2 * u], mx[2 * u + 1])
                l_scr[g, blk["out"], :] = jnp.where(lo, sm[2 * u], sm[2 * u + 1])
                acc_scr[g, blk["out"], :] = jnp.where(lo, o[2 * u], o[2 * u + 1])
            return carry

        lax.fori_loop(0, seq // BAND // ATTN_PAR, body, 0)

    n_g = len(DIL_GROUPS)
    m_all = m_scr[0]
    for g in range(1, n_g):
        m_all = jnp.maximum(m_all, m_scr[g])
    num = jnp.zeros((seq, LANES), F32)
    den = jnp.zeros((seq, LANES), F32)
    for g in range(n_g):
        w = jnp.exp(m_scr[g] - m_all)
        num = num + w * acc_scr[g]
        den = den + w * l_scr[g]
    oa_ref[0] = _head_pair_norm(num / den, og_ref[...], lo).astype(BF16)


def _attn_prompt(proj, bias_tiles, qg, kg, og):
    B, S, _ = proj.shape
    n_pair = MIX_A // LANES
    n_g = len(DIL_GROUPS)
    blk = lambda off: pl.BlockSpec((1, S, LANES), lambda b, hp: (b, 0, off + hp))
    vec = pl.BlockSpec((1, LANES), lambda b, hp: (0, 0))
    t_blk = pl.BlockSpec((1, 2, HEAD_DIM, S), lambda b, hp: (b, hp, 0, 0))
    return pl.pallas_call(
        functools.partial(_attn_p_kernel, seq=S),
        grid=(B, n_pair),
        in_specs=[blk(0), blk(n_pair), blk(2 * n_pair),
                  pl.BlockSpec((n_g, 2, BAND, 2 * BAND), lambda b, hp: (0, hp, 0, 0)),
                  vec, vec, vec],
        out_specs=[t_blk, t_blk, pl.BlockSpec((1, S, LANES), lambda b, hp: (b, 0, hp))],
        out_shape=[jax.ShapeDtypeStruct((B, H_A, HEAD_DIM, S), F32),
                   jax.ShapeDtypeStruct((B, H_A, HEAD_DIM, S), F32),
                   jax.ShapeDtypeStruct((B, S, MIX_A), BF16)],
        scratch_shapes=([pltpu.VMEM((S, LANES), F32)] * 2 + [pltpu.VMEM((n_g - 1, 3, S, LANES), F32)]
                        + [pltpu.VMEM((n_g, S, LANES), F32)] * 3),
        compiler_params=_cparams("parallel", "parallel"),
        name="attn_prompt",
    )(proj, proj, proj, bias_tiles, qg, kg, og)


def _gla_p_kernel(q_ref, k_ref, v_ref, la_ref, r_ref, g_ref, ob_ref, st_ref, st_scr, *, seq):
    C = GLA_BLOCK
    n_sub = C // GLA_SUB
    st_scr[...] = jnp.zeros(st_scr.shape, F32)
    row = lax.broadcasted_iota(jnp.int32, (C, C), 0)
    tri = row >= lax.broadcasted_iota(jnp.int32, (C, C), 1)
    tri_b = jnp.where(tri, 1.0, 0.0).astype(BF16)
    krow = lax.broadcasted_iota(jnp.int32, (C, DK_B), 0)

    heads = range(GLA_PAR)
    ksl = [slice(DK_B * h, DK_B * (h + 1)) for h in heads]
    vsl = [slice(DV_B * h, DV_B * (h + 1)) for h in heads]

    def block(c, carry):
        rs = pl.ds(pl.multiple_of(c * C, C), C)
        la = [la_ref[0, rs, ksl[h]] for h in heads]
        hi = [x.astype(BF16) for x in la]
        r1 = [x - y.astype(F32) for x, y in zip(la, hi)]
        mid = [x.astype(BF16) for x in r1]
        low = [(x - y.astype(F32)).astype(BF16) for x, y in zip(r1, mid)]
        cum = [_dot(tri_b, a) + _dot(tri_b, b) + _dot(tri_b, d) for a, b, d in zip(hi, mid, low)]
        q = [q_ref[0, rs, ksl[h]] for h in heads]
        k = [k_ref[0, rs, ksl[h]] for h in heads]
        v = [v_ref[0, rs, vsl[h]] for h in heads]
        starts = [[(cum[h] - la[h])[GLA_SUB * i:GLA_SUB * i + 1, :] for i in range(n_sub)] for h in heads]
        start_rows = [jnp.concatenate([jnp.broadcast_to(s, (GLA_SUB, DK_B)) for s in starts[h]], axis=0)
                      for h in heads]
        qe = [(q[h] * jnp.exp(cum[h] - start_rows[h])).astype(BF16) for h in heads]
        khat = [[(k[h] * jnp.exp(jnp.where(krow < GLA_SUB * (i + 1), starts[h][i] - cum[h], NEG))).astype(BF16)
                 for i in range(n_sub)] for h in heads]
        att = [jnp.concatenate([_dot_nt(qe[h][GLA_SUB * i:GLA_SUB * (i + 1)], khat[h][i]) for i in range(n_sub)],
                               axis=0) for h in heads]
        att = [jnp.where(tri, x, 0.0).astype(BF16) for x in att]
        o = [_dot(att[h], v[h].astype(BF16))
             + _dot_nt((q[h] * jnp.exp(cum[h])).astype(BF16), st_scr[h].astype(BF16)) for h in heads]
        last = [x[C - 1:C, :] for x in cum]
        kd = [(k[h] * jnp.exp(last[h] - cum[h])).astype(BF16) for h in heads]
        upd = [_dot(v[h].T.astype(BF16), kd[h]) for h in heads]
        for h in heads:
            st_scr[h] = st_scr[h] * jnp.exp(last[h]) + upd[h]
            ob_ref[0, rs, vsl[h]] = (_rms(o[h], g_ref[...]) * _silu(r_ref[0, rs, vsl[h]])).astype(BF16)
        return carry

    lax.fori_loop(0, seq // C, block, 0, unroll=GLA_UNROLL)
    for h in heads:
        st_ref[0, h] = st_scr[h].T


def _gla_prompt(proj, log_a, gla_g):
    B, S, _ = proj.shape
    kw, vw = GLA_PAR * DK_B, GLA_PAR * DV_B
    q0 = 3 * MIX_A // kw
    k0 = q0 + KEY_B // kw
    v0 = (3 * MIX_A + 2 * KEY_B) // vw
    r0 = v0 + MIX_B // vw
    kblk = lambda off: pl.BlockSpec((1, S, kw), lambda b, h: (b, 0, off + h))
    vblk = lambda off: pl.BlockSpec((1, S, vw), lambda b, h: (b, 0, off + h))
    return pl.pallas_call(
        functools.partial(_gla_p_kernel, seq=S),
        grid=(B, H_B // GLA_PAR),
        in_specs=[kblk(q0), kblk(k0), vblk(v0), kblk(0), vblk(r0),
                  pl.BlockSpec((1, DV_B), lambda b, h: (0, 0))],
        out_specs=[vblk(0), pl.BlockSpec((1, GLA_PAR, DK_B, DV_B), lambda b, h: (b, h, 0, 0))],
        out_shape=[jax.ShapeDtypeStruct((B, S, MIX_B), BF16),
                   jax.ShapeDtypeStruct((B, H_B, DK_B, DV_B), F32)],
        scratch_shapes=[pltpu.VMEM((GLA_PAR, DV_B, DK_B), F32)],
        compiler_params=_cparams("parallel", "parallel"),
        name="gla_prompt",
    )(proj, proj, proj, log_a, proj, gla_g)


def _gla_s_kernel(q_ref, k_ref, v_ref, la_ref, r_ref, g_ref, s_ref, ob_ref, so_ref, *, steps):
    rowi = lax.broadcasted_iota(jnp.int32, (DK_B, DK_B), 0)
    for h in range(H_B):
        ks = slice(DK_B * h, DK_B * (h + 1))
        vs = slice(DV_B * h, DV_B * (h + 1))
        x = jnp.zeros((DK_B, DK_B), F32)
        for t in range(steps):
            x = jnp.where(rowi == t, k_ref[0, t:t + 1, ks], x)
            x = jnp.where(rowi == steps + t, jnp.exp(la_ref[0, t:t + 1, ks]), x)
            x = jnp.where(rowi == 2 * steps + t, q_ref[0, t:t + 1, ks], x)
        xt = x.T
        st = s_ref[0, h]
        for t in range(steps):
            st = xt[:, steps + t:steps + t + 1] * st + xt[:, t:t + 1] * v_ref[0, t:t + 1, vs]
            o = jnp.sum(xt[:, 2 * steps + t:2 * steps + t + 1] * st, axis=0, keepdims=True)
            ob_ref[0, t:t + 1, vs] = _rms(o, g_ref[...]) * _silu(r_ref[0, t:t + 1, vs])
        so_ref[0, h] = st


def _gla_sample(l, q, k, v, log_a, r, gla_g, state):
    B, T, _ = q.shape
    kb = pl.BlockSpec((1, T, KEY_B), lambda b: (b, 0, 0))
    vb = pl.BlockSpec((1, T, MIX_B), lambda b: (b, 0, 0))
    return pl.pallas_call(
        functools.partial(_gla_s_kernel, steps=T),
        grid=(B,),
        in_specs=[kb, kb, vb, kb, vb, pl.BlockSpec((1, DV_B), lambda b: (0, 0)),
                  pl.BlockSpec((None, 1, H_B, DK_B, DV_B), lambda b: (l, b, 0, 0, 0))],
        out_specs=[vb, pl.BlockSpec((1, H_B, DK_B, DV_B), lambda b: (b, 0, 0, 0))],
        out_shape=[jax.ShapeDtypeStruct((B, T, MIX_B), F32),
                   jax.ShapeDtypeStruct((B, H_B, DK_B, DV_B), F32)],
        compiler_params=_cparams("parallel"),
        name="gla_sample",
    )(q, k, v, log_a, r, gla_g, state)


def _attn_s_kernel(q_ref, k_ref, v_ref, kt_ref, vt_ref, bnear_ref, bfar_ref, bnew_ref,
                   qg_ref, kg_ref, og_ref, kn_ref, oa_ref, kpad_scr, vpad_scr, *, n_near):
    n_head, n_row, _ = kn_ref.shape[1:]
    width = kt_ref.shape[-1]
    kn = _rms(k_ref[0], kg_ref[...])
    kn_ref[0] = kn
    qs = _rms(q_ref[0], qg_ref[...]) * ATTN_SCALE
    kpad_scr[...] = jnp.zeros(kpad_scr.shape, F32)
    vpad_scr[...] = jnp.zeros(vpad_scr.shape, F32)
    kpad_scr[:, 0:n_row, :] = kn
    vpad_scr[:, 0:n_row, :] = v_ref[0]
    n_g = len(DIL_GROUPS)
    for h in range(n_head):
        qh = qs[h].astype(BF16)
        s = _dot(qh, kt_ref[h].astype(BF16))
        s_new = _dot_nt(qh, kpad_scr[h].astype(BF16))
        far = s + bfar_ref[h]
        s_near = s[:, width - n_near:]
        near = [s_near + bnear_ref[g, h] for g in range(n_g - 1)]
        new = [s_new + bnew_ref[g, h] for g in range(n_g)]
        m = jnp.max(far, axis=-1, keepdims=True)
        for x in near + new:
            m = jnp.maximum(m, jnp.max(x, axis=-1, keepdims=True))
        p_far = jnp.exp(far - m)
        p_near = sum(jnp.exp(x - m) for x in near)
        p_new = sum(jnp.exp(x - m) for x in new)
        den = (jnp.sum(p_far, axis=-1, keepdims=True) + jnp.sum(p_near, axis=-1, keepdims=True)
               + jnp.sum(p_new, axis=-1, keepdims=True))
        p = jnp.concatenate([p_far[:, :width - n_near], p_far[:, width - n_near:] + p_near], axis=1)
        o = _dot_nt(p.astype(BF16), vt_ref[h].astype(BF16)) + _dot(p_new.astype(BF16), vpad_scr[h].astype(BF16))
        oa_ref[0, h] = _rms(o / den, og_ref[...])


def _attn_sample(l, qkv, cache_kt, cache_vt, bias_near, bias_far, bias_new, qg, kg, og):
    _, B, n_head, n_row, _ = qkv.shape
    W = cache_kt.shape[-1]
    n_near = bias_near.shape[-1]
    qblk = lambda i: pl.BlockSpec((None, 1, n_head, n_row, HEAD_DIM), lambda b: (i, b, 0, 0, 0))
    cblk = pl.BlockSpec((None, None, n_head, HEAD_DIM, W), lambda b: (l, b, 0, 0, 0))
    full = lambda a: pl.BlockSpec(a.shape, lambda b: (0,) * a.ndim)
    vec = pl.BlockSpec((1, HEAD_DIM), lambda b: (0, 0))
    out_blk = pl.BlockSpec((1, n_head, n_row, HEAD_DIM), lambda b: (b, 0, 0, 0))
    return pl.pallas_call(
        functools.partial(_attn_s_kernel, n_near=n_near),
        grid=(B,),
        in_specs=[qblk(0), qblk(1), qblk(2), cblk, cblk,
                  full(bias_near), full(bias_far), full(bias_new), vec, vec, vec],
        out_specs=[out_blk, out_blk],
        out_shape=[jax.ShapeDtypeStruct((B, n_head, n_row, HEAD_DIM), F32),
                   jax.ShapeDtypeStruct((B, n_head, n_row, HEAD_DIM), F32)],
        scratch_shapes=[pltpu.VMEM((n_head, BAND, HEAD_DIM), F32)] * 2,
        compiler_params=_cparams("parallel"),
        name="attn_sample",
    )(qkv, qkv, qkv, cache_kt, cache_vt, bias_near, bias_far, bias_new, qg, kg, og)


def _bucket_table(n):
    dist = np.arange(n)
    exact = N_BUCKETS // 2
    df = np.maximum(dist, 1).astype(np.float32)
    large = exact + (np.log(df / np.float32(exact)) / np.float32(math.log(MAX_DISTANCE / exact))
                     * np.float32(N_BUCKETS - exact)).astype(np.int32)
    return np.where(dist < exact, dist, np.minimum(large, N_BUCKETS - 1))


def _prompt_bias_tiles(rel_bias):
    span = 3 * BAND
    tiles = []
    for _, dil in DIL_GROUPS:
        steps = np.arange(BAND, -1, -1)
        vec = rel_bias[_bucket_table(BAND * dil + 1)[steps * dil]].T.astype(F32)
        u = jnp.concatenate([vec, jnp.full((H_A, span - BAND - 1), NEG, F32)], axis=1)
        t = jnp.tile(u, (1, BAND))[:, :BAND * (span - 1)].reshape(H_A, BAND, span - 1)
        tiles.append(t[:, :, :2 * BAND])
    return jnp.stack(tiles)


def _sample_bias_tiles(rel_bias, steps, rows, wbuf):
    n_dist = wbuf + rows
    bucket = _bucket_table(n_dist)
    dist = np.arange(n_dist)
    tab = rel_bias[bucket].T.astype(F32)
    n_near = DIL_GROUPS[-2][0]
    tiles = []
    for win, dil in DIL_GROUPS:
        ok = (dist % dil == 0) & (dist >= dil) & (dist <= win)
        rev = jnp.where(ok[None, ::-1], tab[:, ::-1], NEG)
        tiles.append(jnp.stack([rev[:, rows - 1 - t:rows - 1 - t + wbuf] for t in range(rows)], axis=1))
    near = jnp.stack([t[:, :, wbuf - n_near:] for t in tiles[:-1]])
    far = tiles[-1]
    t_new = np.arange(rows)[:, None]
    tau = np.arange(BAND)[None, :]
    live = (t_new < steps) & (tau < steps)
    back = np.clip(t_new - tau, 0, steps)
    new = []
    for g, (_, dil) in enumerate(DIL_GROUPS):
        ok = live & ((tau <= t_new) if g == 0 else (tau == t_new))
        new.append(jnp.where(ok[None], rel_bias[bucket[back * dil]].transpose(2, 0, 1).astype(F32), NEG))
    return near, far, jnp.stack(new)


def kernel(x_prompt, x_sample, cache_k_win, cache_v_win, state_gla, state_conv, c_prompt, c_sample,
           rel_bias, norm1_g, norm2_g, w_ada, b_ada, w_in, q_norm_g, k_norm_g, w_gate2, b_gate2,
           a_out_g, gla_out_g, w_out, w_up, conv_w, conv_b, w_down):
    B, S, _ = x_prompt.shape
    Bs, T, _ = x_sample.shape
    wbuf = cache_k_win.shape[2]
    Mp, Ms = B * S, Bs * T
    assert wbuf == DIL_GROUPS[-1][0] and T <= SUBLANES

    w_main = w_in.astype(BF16)
    w_glr = jnp.pad(w_in[:, :, MAIN_COLS:], ((0, 0), (0, 0), (0, LANES - GATE_RANK))).astype(BF16)
    w_g2 = jnp.pad(w_gate2, ((0, 0), (0, LANES - GATE_RANK), (0, 0))).astype(BF16)
    b_g2 = b_gate2.reshape(DEPTH, 1, KEY_B)
    w_out_b = w_out.astype(BF16)
    w_up_b = w_up.astype(BF16)
    w_down_b = w_down.astype(BF16)
    col_scale = jnp.ones((1, MAIN_COLS), F32).at[:, 3 * MIX_A:3 * MIX_A + KEY_B].set(DK_B ** -0.5)
    vec3 = lambda a: a.reshape(DEPTH, 1, -1)
    pair = lambda a, l: jnp.tile(a[l], LANES // HEAD_DIM).reshape(1, LANES)
    head = lambda a, l: a[l].reshape(1, HEAD_DIM)

    mods = _ada_mods(jnp.concatenate([c_prompt, c_sample], axis=0), w_ada, b_ada)

    bias_p = _prompt_bias_tiles(rel_bias)
    bias_near, bias_far, bias_new = _sample_bias_tiles(rel_bias, T, SUBLANES, wbuf)

    cache_kt = jnp.transpose(cache_k_win, (0, 1, 3, 4, 2))
    cache_vt = jnp.transpose(cache_v_win, (0, 1, 3, 4, 2))
    hist_p = jnp.zeros((B, CONV_W - 1, 1, D_FF), F32)

    xp = x_prompt.reshape(Mp, D_MODEL)
    xs = jnp.transpose(x_sample, (1, 0, 2)).reshape(Ms, D_MODEL)
    outs = [[] for _ in range(8)]
    for l in range(DEPTH):
        mod_p = [m.reshape(B, 1, D_MODEL) for m in jnp.split(mods[l, :B], 6, axis=-1)]
        mod_s = [jnp.tile(m, (T, 1)).reshape(1, Ms, D_MODEL) for m in jnp.split(mods[l, B:], 6, axis=-1)]

        proj, log_a = _in_proj(l, xp, vec3(norm1_g), mod_p[1], mod_p[0], w_main, col_scale, w_glr, w_g2, b_g2,
                               tm=TM_IN, rows_per_group=S)
        proj = proj.reshape(B, S, MAIN_COLS)
        knt, vt, oa = _attn_prompt(proj, bias_p, pair(q_norm_g, l), pair(k_norm_g, l), pair(a_out_g, l))
        ob, g_st = _gla_prompt(proj, log_a.reshape(B, S, KEY_B), gla_out_g[l].reshape(1, DV_B))
        xp, h2 = _out_proj(l, oa.reshape(Mp, MIX_A), ob.reshape(Mp, MIX_B), xp, mod_p[2], w_out_b,
                           vec3(norm2_g), mod_p[4], mod_p[3], tm=TM_OUT, rows_per_group=S)
        act, tail = _up_proj(l, h2, w_up_b, conv_w, vec3(conv_b), hist_p, n_seq=1, steps=S, tn=TN_UP)
        xp = _down_proj(l, act, w_down_b, xp, mod_p[5], tm=TM_DOWN, tn=TN_DOWN, rows_per_group=S)
        outs[0].append(knt)
        outs[1].append(vt)
        outs[2].append(g_st)
        outs[3].append(tail[:, tail.shape[1] - (CONV_W - 1):])

        proj, log_a = _in_proj(l, xs, vec3(norm1_g), mod_s[1], mod_s[0], w_main, col_scale, w_glr, w_g2, b_g2,
                               tm=Ms, rows_per_group=Ms)
        bm = lambda a: jnp.transpose(a.reshape(T, Bs, -1), (1, 0, 2))
        qkv = jnp.transpose(proj[:, :3 * MIX_A].reshape(T, Bs, 3, H_A, HEAD_DIM), (2, 1, 3, 0, 4))
        qkv_pad = jnp.pad(qkv, ((0, 0), (0, 0), (0, 0), (0, SUBLANES - T), (0, 0)))
        kn_s, oa_s = _attn_sample(l, qkv_pad, cache_kt, cache_vt, bias_near, bias_far, bias_new,
                                  head(q_norm_g, l), head(k_norm_g, l), head(a_out_g, l))
        gla_in = bm(proj[:, 3 * MIX_A:])
        ob_s, g_st = _gla_sample(l, gla_in[..., :KEY_B], gla_in[..., KEY_B:2 * KEY_B],
                                 gla_in[..., 2 * KEY_B:2 * KEY_B + MIX_B], bm(log_a),
                                 gla_in[..., 2 * KEY_B + MIX_B:], gla_out_g[l].reshape(1, DV_B), state_gla)
        oa_t = jnp.transpose(oa_s[:, :, :T], (2, 0, 1, 3)).reshape(Ms, MIX_A).astype(BF16)
        ob_t = jnp.transpose(ob_s, (1, 0, 2)).reshape(Ms, MIX_B).astype(BF16)
        xs, h2 = _out_proj(l, oa_t, ob_t, xs, mod_s[2], w_out_b,
                           vec3(norm2_g), mod_s[4], mod_s[3], tm=Ms, rows_per_group=Ms)
        hist_s = jnp.transpose(state_conv[l], (1, 0, 2)).reshape(1, CONV_W - 1, Bs, D_FF)
        act, tail = _up_proj(l, h2, w_up_b, conv_w, vec3(conv_b), hist_s, n_seq=Bs, steps=T, tn=TN_UP)
        xs = _down_proj(l, act, w_down_b, xs, mod_s[5], tm=Ms, tn=TN_DOWN, rows_per_group=Ms)
        outs[4].append(jnp.transpose(kn_s[:, :, :T], (0, 2, 1, 3)))
        outs[5].append(jnp.transpose(qkv[2], (0, 2, 1, 3)))
        outs[6].append(g_st)
        outs[7].append(jnp.transpose(tail.reshape(CONV_W - 1, Bs, D_FF), (1, 0, 2)))

    y_prompt = xp.reshape(B, S, D_MODEL)
    y_sample = jnp.transpose(xs.reshape(T, Bs, D_MODEL), (1, 0, 2))
    stacked = [jnp.stack(o) for o in outs]
    stacked[0] = jnp.transpose(stacked[0], (0, 1, 4, 2, 3))
    stacked[1] = jnp.transpose(stacked[1], (0, 1, 4, 2, 3))
    return (y_prompt, y_sample) + tuple(stacked)
```

```python
import functools
import math

import numpy as np
import jax
import jax.numpy as jnp
from jax import lax
from jax.experimental import pallas as pl
from jax.experimental.pallas import tpu as pltpu

F32 = jnp.float32
BF16 = jnp.bfloat16

D_MODEL = 2048
DEPTH = 2
HEAD_DIM = 64
MIX_A = D_MODEL // 2
H_A = MIX_A // HEAD_DIM
DIL_GROUPS = ((128, 1), (512, 4), (2048, 16))
N_BUCKETS = 32
MAX_DISTANCE = 2048
MIX_B = D_MODEL - MIX_A
H_B = 4
DV_B = MIX_B // H_B
DK_B = DV_B // 2
KEY_B = H_B * DK_B
GATE_RANK = 16
GATE_TAU = 16.0
D_FF = ((8 * D_MODEL // 3 + 255) // 256) * 256
CONV_W = 3
EPS = 1e-6
ATTN_SCALE = HEAD_DIM ** -0.5
MAIN_COLS = 3 * MIX_A + 2 * KEY_B + 2 * MIX_B
BAND = 128
NEG = -1e30

LANES = 128
SUBLANES = 8
VMEM_LIMIT = 56 * 1024 * 1024
TM_IN, TM_OUT, TM_DOWN = 1024, 512, 1024
TN_UP, TN_DOWN = 512, 512
UP_SPLIT, OUT_SPLIT = 2, 2
GLA_BLOCK = 128
GLA_SUB = 16
ATTN_PAR = 8
GLA_UNROLL = 2
GLA_PAR = 2

_NT = (((1,), (1,)), ((), ()))


def _cparams(*sem):
    return pltpu.CompilerParams(dimension_semantics=sem, vmem_limit_bytes=VMEM_LIMIT)


def _dot(a, b):
    return jnp.dot(a, b, preferred_element_type=F32)


def _dot_nt(a, b):
    return lax.dot_general(a, b, _NT, preferred_element_type=F32)


def _silu(x):
    return x * jax.nn.sigmoid(x)


def _rms(x, g):
    return x * lax.rsqrt(jnp.mean(x * x, axis=-1, keepdims=True) + EPS) * g


def _ada_kernel(c_ref, w_ref, b_ref, o_ref):
    a = _silu(c_ref[...]).astype(BF16)
    o_ref[...] = _dot(a, w_ref[...].astype(BF16)) + b_ref[...]


def _ada_mods(c_all, w_ada, b_ada):
    G = c_all.shape[0]
    tn = 1024
    return pl.pallas_call(
        _ada_kernel,
        grid=(DEPTH, 6 * D_MODEL // tn),
        in_specs=[pl.BlockSpec((G, D_MODEL), lambda l, j: (0, 0)),
                  pl.BlockSpec((None, D_MODEL, tn), lambda l, j: (l, 0, j)),
                  pl.BlockSpec((None, 1, tn), lambda l, j: (l, 0, j))],
        out_specs=pl.BlockSpec((None, G, tn), lambda l, j: (l, 0, j)),
        out_shape=jax.ShapeDtypeStruct((DEPTH, G, 6 * D_MODEL), F32),
        compiler_params=_cparams("parallel", "parallel"),
        name="ada_mods",
    )(c_all, w_ada, b_ada.reshape(DEPTH, 1, 6 * D_MODEL))


def _cast_kernel(x_ref, o_ref):
    o_ref[...] = x_ref[...].astype(BF16)


def _main_cols_bf16(w_in):
    tn = 1024
    return pl.pallas_call(
        _cast_kernel,
        grid=(DEPTH, MAIN_COLS // tn),
        in_specs=[pl.BlockSpec((None, D_MODEL, tn), lambda l, j: (l, 0, j))],
        out_specs=pl.BlockSpec((None, D_MODEL, tn), lambda l, j: (l, 0, j)),
        out_shape=jax.ShapeDtypeStruct((DEPTH, D_MODEL, MAIN_COLS), BF16),
        compiler_params=_cparams("parallel", "parallel"),
        name="w_in_bf16",
    )(w_in)


def _in_kernel(x_ref, g_ref, sc_ref, sh_ref, w_ref, cs_ref, wg_ref, wg2_ref, bg2_ref,
               o_ref, la_ref, h_scr):
    @pl.when(pl.program_id(1) == 0)
    def _():
        hb = (_rms(x_ref[...], g_ref[...]) * (1.0 + sc_ref[...]) + sh_ref[...]).astype(BF16)
        h_scr[...] = hb
        glr = _dot(hb, wg_ref[...])
        z = _dot(glr.astype(BF16), wg2_ref[...]) + bg2_ref[...]
        la_ref[...] = (jnp.minimum(z, 0.0) - jnp.log(1.0 + jnp.exp(-jnp.abs(z)))) * (1.0 / GATE_TAU)

    o_ref[...] = _dot(h_scr[...], w_ref[...]) * cs_ref[...]


def _in_proj(l, x, norm_g, scale, shift, w_main, col_scale, w_glr, w_g2, b_g2, tm, rows_per_group):
    M = x.shape[0]
    tn = 1024
    R = scale.shape[1]
    grp = lambda i, j: ((i * tm) // rows_per_group, 0, 0)
    return pl.pallas_call(
        _in_kernel,
        grid=(M // tm, MAIN_COLS // tn),
        in_specs=[pl.BlockSpec((tm, D_MODEL), lambda i, j: (i, 0)),
                  pl.BlockSpec((None, 1, D_MODEL), lambda i, j: (l, 0, 0)),
                  pl.BlockSpec((None, R, D_MODEL), grp),
                  pl.BlockSpec((None, R, D_MODEL), grp),
                  pl.BlockSpec((None, D_MODEL, tn), lambda i, j: (l, 0, j)),
                  pl.BlockSpec((1, tn), lambda i, j: (0, j)),
                  pl.BlockSpec((None, D_MODEL, LANES), lambda i, j: (l, 0, 0)),
                  pl.BlockSpec((None, LANES, KEY_B), lambda i, j: (l, 0, 0)),
                  pl.BlockSpec((None, 1, KEY_B), lambda i, j: (l, 0, 0))],
        out_specs=[pl.BlockSpec((tm, tn), lambda i, j: (i, j)),
                   pl.BlockSpec((tm, KEY_B), lambda i, j: (i, 0))],
        out_shape=[jax.ShapeDtypeStruct((M, MAIN_COLS), F32),
                   jax.ShapeDtypeStruct((M, KEY_B), F32)],
        scratch_shapes=[pltpu.VMEM((tm, D_MODEL), BF16)],
        compiler_params=_cparams("parallel", "arbitrary"),
        name="in_proj",
    )(x, norm_g, scale, shift, w_main, col_scale, w_glr, w_g2, b_g2)


def _out_kernel(oa_ref, ob_ref, x_ref, g1_ref, wa_ref, wb_ref, n2_ref, sc_ref, sh_ref, xo_ref, h2_ref):
    tm = x_ref.shape[0]
    halves = [pl.ds(c * (tm // OUT_SPLIT), tm // OUT_SPLIT) for c in range(OUT_SPLIT)]
    ys = [_dot(oa_ref[rs, :], wa_ref[...]) + _dot(ob_ref[rs, :], wb_ref[...]) for rs in halves]
    per_row = g1_ref.shape[0] == tm
    for rs, y in zip(halves, ys):
        mod = lambda ref: ref[rs, :] if per_row else ref[...]
        xn = x_ref[rs, :] + mod(g1_ref) * y
        xo_ref[rs, :] = xn
        h2_ref[rs, :] = (_rms(xn, n2_ref[...]) * (1.0 + mod(sc_ref)) + mod(sh_ref)).astype(BF16)


def _out_proj(l, oa, ob, x, gate1, w_out, norm2_g, scale2, shift2, tm, rows_per_group):
    M = x.shape[0]
    R = gate1.shape[1]
    grp = lambda i: ((i * tm) // rows_per_group, 0, 0)
    return pl.pallas_call(
        _out_kernel,
        grid=(M // tm,),
        in_specs=[pl.BlockSpec((tm, MIX_A), lambda i: (i, 0)),
                  pl.BlockSpec((tm, MIX_B), lambda i: (i, 0)),
                  pl.BlockSpec((tm, D_MODEL), lambda i: (i, 0)),
                  pl.BlockSpec((None, R, D_MODEL), grp),
                  pl.BlockSpec((None, MIX_A, D_MODEL), lambda i: (l, 0, 0)),
                  pl.BlockSpec((None, MIX_B, D_MODEL), lambda i: (l, 1, 0)),
                  pl.BlockSpec((None, 1, D_MODEL), lambda i: (l, 0, 0)),
                  pl.BlockSpec((None, R, D_MODEL), grp),
                  pl.BlockSpec((None, R, D_MODEL), grp)],
        out_specs=[pl.BlockSpec((tm, D_MODEL), lambda i: (i, 0)),
                   pl.BlockSpec((tm, D_MODEL), lambda i: (i, 0))],
        out_shape=[jax.ShapeDtypeStruct((M, D_MODEL), F32),
                   jax.ShapeDtypeStruct((M, D_MODEL), BF16)],
        compiler_params=_cparams("parallel"),
        name="out_proj",
    )(oa, ob, x, gate1, w_out, w_out, norm2_g, scale2, shift2)


def _up_kernel(h_ref, wg_ref, wu_ref, cw_ref, cb_ref, hist_ref, act_ref, tail_ref, *, n_seq, steps):
    h = h_ref[...]
    rows = n_seq * steps
    tail = tail_ref.shape[0]
    tn = act_ref.shape[1]
    halves = [slice(c * (tn // UP_SPLIT), (c + 1) * (tn // UP_SPLIT)) for c in range(UP_SPLIT)]
    gs = [_dot(h, wg_ref[:, cs]) for cs in halves]
    us = [_dot(h, wu_ref[:, cs]) for cs in halves]
    for cs, g, u in zip(halves, gs, us):
        row = lax.broadcasted_iota(jnp.int32, g.shape, 0)

        def per_row(hist, g=g):
            if n_seq == 1:
                return jnp.broadcast_to(hist, g.shape)
            return jnp.concatenate([hist] * steps, axis=0)

        h0 = per_row(hist_ref[0, :, cs])
        h1 = per_row(hist_ref[1, :, cs])
        g1 = jnp.where(row < n_seq, h1, pltpu.roll(g, n_seq, 0))
        g2 = jnp.where(row < n_seq, h0, jnp.where(row < 2 * n_seq, h1, pltpu.roll(g, 2 * n_seq, 0)))
        cw = cw_ref[:, cs]
        conv = cb_ref[:, cs] + cw[0:1] * g2 + cw[1:2] * g1 + cw[2:3] * g
        act_ref[:, cs] = (_silu(conv) * u).astype(BF16)
        tail_ref[:, cs] = g[rows - tail:]


def _up_proj(l, h2, w_up, conv_w, conv_b, hist, n_seq, steps, tn):
    M = h2.shape[0]
    tm = n_seq * steps
    nj = D_FF // tn
    tail = max(SUBLANES, (CONV_W - 1) * n_seq)
    return pl.pallas_call(
        functools.partial(_up_kernel, n_seq=n_seq, steps=steps),
        grid=(M // tm, nj),
        in_specs=[pl.BlockSpec((tm, D_MODEL), lambda i, j: (i, 0)),
                  pl.BlockSpec((None, D_MODEL, tn), lambda i, j: (l, 0, j)),
                  pl.BlockSpec((None, D_MODEL, tn), lambda i, j: (l, 0, nj + j)),
                  pl.BlockSpec((None, CONV_W, tn), lambda i, j: (l, 0, j)),
                  pl.BlockSpec((None, 1, tn), lambda i, j: (l, 0, j)),
                  pl.BlockSpec((None, CONV_W - 1, n_seq, tn), lambda i, j: (i, 0, 0, j))],
        out_specs=[pl.BlockSpec((tm, tn), lambda i, j: (i, j)),
                   pl.BlockSpec((None, tail, tn), lambda i, j: (i, 0, j))],
        out_shape=[jax.ShapeDtypeStruct((M, D_FF), BF16),
                   jax.ShapeDtypeStruct((M // tm, tail, D_FF), F32)],
        compiler_params=_cparams("parallel", "parallel"),
        name="up_proj",
    )(h2, w_up, w_up, conv_w, conv_b, hist)


def _down_kernel(a_ref, w_ref, x_ref, g2_ref, o_ref):
    o_ref[...] = x_ref[...] + g2_ref[...] * _dot(a_ref[...], w_ref[...])


def _down_proj(l, act, w_down, x, gate2, tm, tn, rows_per_group):
    M = x.shape[0]
    R = gate2.shape[1]
    return pl.pallas_call(
        _down_kernel,
        grid=(M // tm, D_MODEL // tn),
        in_specs=[pl.BlockSpec((tm, D_FF), lambda i, j: (i, 0)),
                  pl.BlockSpec((None, D_FF, tn), lambda i, j: (l, 0, j)),
                  pl.BlockSpec((tm, tn), lambda i, j: (i, j)),
                  pl.BlockSpec((None, R, tn), lambda i, j: ((i * tm) // rows_per_group, 0, j))],
        out_specs=pl.BlockSpec((tm, tn), lambda i, j: (i, j)),
        out_shape=jax.ShapeDtypeStruct((M, D_MODEL), F32),
        compiler_params=_cparams("parallel", "parallel"),
        name="down_proj",
    )(act, w_down, x, gate2)


def _head_pair_norm(x, g, same_head):
    x2 = x * x
    hi = x2.astype(BF16)
    low = (x2 - hi.astype(F32)).astype(BF16)
    ms = (_dot(hi, same_head) + _dot(low, same_head)) * (1.0 / HEAD_DIM)
    return x * lax.rsqrt(ms + EPS) * g


def _attn_p_kernel(q_ref, k_ref, v_ref, bias_ref, qg_ref, kg_ref, og_ref, *rest, seq, n_alias):
    knt_ref, vt_ref, oa_ref, qs_scr, ks_scr, perm_scr, m_scr, l_scr, acc_scr = rest[n_alias:]
    lo =lax.broadcasted_iota(jnp.int32, (1, LANES), 1) < HEAD_DIM
    head_of = lambda axis: lax.broadcasted_iota(jnp.int32, (LANES, LANES), axis) // HEAD_DIM
    same_head = jnp.where(head_of(0) == head_of(1), 1.0, 0.0).astype(BF16)
    kn = _head_pair_norm(k_ref[0], kg_ref[...], same_head)
    ks_scr[...] = kn
    for ref, val in ((knt_ref, kn), (vt_ref, v_ref[0])):
        if n_alias:
            ref[0] = val.T.reshape(2, HEAD_DIM, seq)
        else:
            ref[0, 0] = val.T.reshape(2, HEAD_DIM, seq)
            ref[1:, 0] = jnp.zeros((ref.shape[0] - 1, 2, HEAD_DIM, seq), F32)
    qs_scr[...] = _head_pair_norm(q_ref[0], qg_ref[...], same_head) * ATTN_SCALE
    col = lax.broadcasted_iota(jnp.int32, (BAND, 2 * BAND), 1)

    stage_rows = [(lambda rs: qs_scr[rs, :], lambda rs: ks_scr[rs, :], lambda rs: v_ref[0, rs, :])]
    for g in range(1, len(DIL_GROUPS)):
        d_prev, d_cur = DIL_GROUPS[g - 1][1], DIL_GROUPS[g][1]
        ratio = d_cur // d_prev
        len_prev, len_cur = seq // d_prev, seq // d_cur
        for t in range(3):
            for c in range(d_prev):
                for m in range(ratio):
                    src = pl.ds(c * len_prev + m, len_cur, stride=ratio)
                    perm_scr[g - 1, t, pl.ds((c + d_prev * m) * len_cur, len_cur), :] = stage_rows[g - 1][t](src)
        stage_rows.append(tuple((lambda rs, g=g, t=t: perm_scr[g - 1, t, rs, :]) for t in range(3)))

    for g, (_, dil) in enumerate(DIL_GROUPS):
        n_blk = seq // dil // BAND
        has_prev = n_blk > 1
        q_rows, k_rows, v_rows = stage_rows[g]

        def rows(start, dil=dil):
            return pl.ds(start, BAND, stride=dil) if dil > 1 else pl.ds(start, BAND)

        def body(trip, carry, g=g, dil=dil, n_blk=n_blk, rows=rows, has_prev=has_prev,
                 q_rows=q_rows, k_rows=k_rows, v_rows=v_rows):
            blocks = []
            for u in range(ATTN_PAR):
                idx = trip * ATTN_PAR + u
                r = idx // n_blk
                nb = idx % n_blk
                cur = pl.ds(pl.multiple_of((r * n_blk + nb) * BAND, BAND), BAND)
                qb = q_rows(cur)
                kcat = k_rows(cur)
                vcat = v_rows(cur)
                bias = [bias_ref[g, h, :, BAND:] for h in range(2)]
                if has_prev:
                    prev = pl.ds(pl.multiple_of((r * n_blk + jnp.maximum(nb - 1, 0)) * BAND, BAND), BAND)
                    kcat = jnp.concatenate([k_rows(prev), kcat], axis=0)
                    vcat = jnp.concatenate([v_rows(prev), vcat], axis=0)
                    no_prev = jnp.logical_and(nb == 0, col < BAND)
                    bias = [jnp.where(no_prev, NEG, bias_ref[g, h]) for h in range(2)]
                blocks.append(dict(out=rows(r + nb * (BAND * dil)), q=qb, k=kcat.astype(BF16),
                                   v=vcat.astype(BF16), bias=bias))
            heads = [(blk, h) for blk in blocks for h in range(2)]
            s = [_dot_nt(jnp.where(lo if h == 0 else jnp.logical_not(lo), blk["q"], 0.0).astype(BF16), blk["k"])
                 + blk["bias"][h] for blk, h in heads]
            mx = [jnp.max(x, axis=-1, keepdims=True) for x in s]
            p = [jnp.exp(x - m) for x, m in zip(s, mx)]
            sm = [jnp.sum(x, axis=-1, keepdims=True) for x in p]
            o = [_dot(x.astype(BF16), blk["v"]) for x, (blk, _) in zip(p, heads)]
            for u, blk in enumerate(blocks):
                m_scr[g, blk["out"], :] = jnp.where(lo, mx[2 * u], mx[2 * u + 1])
                l_scr[g, blk["out"], :] = jnp.where(lo, sm[2 * u], sm[2 * u + 1])
                acc_scr[g, blk["out"], :] = jnp.where(lo, o[2 * u], o[2 * u + 1])
            return carry

        lax.fori_loop(0, seq // BAND // ATTN_PAR, body, 0)

    n_g = len(DIL_GROUPS)
    m_all = m_scr[0]
    for g in range(1, n_g):
        m_all = jnp.maximum(m_all, m_scr[g])
    num = jnp.zeros((seq, LANES), F32)
    den = jnp.zeros((seq, LANES), F32)
    for g in range(n_g):
        w = jnp.exp(m_scr[g] - m_all)
        num = num + w * acc_scr[g]
        den = den + w * l_scr[g]
    oa_ref[0] = _head_pair_norm(num / den, og_ref[...], same_head).astype(BF16)


def _attn_prompt(l, proj, bias_tiles, qg, kg, og, stacked):
    B, S, _ = proj.shape
    assert bool(stacked) == (l > 0)
    n_pair = MIX_A // LANES
    n_g = len(DIL_GROUPS)
    blk = lambda off: pl.BlockSpec((1, S, LANES), lambda b, hp: (b, 0, off + hp))
    vec = pl.BlockSpec((1, LANES), lambda b, hp: (0, 0))
    if stacked:
        t_blk = pl.BlockSpec((None, 1, 2, HEAD_DIM, S), lambda b, hp: (l, b, hp, 0, 0))
    else:
        t_blk = pl.BlockSpec((DEPTH, 1, 2, HEAD_DIM, S), lambda b, hp: (0, b, hp, 0, 0))
    n_in = 7
    return pl.pallas_call(
        functools.partial(_attn_p_kernel, seq=S, n_alias=len(stacked)),
        grid=(B, n_pair),
        in_specs=[blk(0), blk(n_pair), blk(2 * n_pair),
                  pl.BlockSpec((n_g, 2, BAND, 2 * BAND), lambda b, hp: (0, hp, 0, 0)),
                  vec, vec, vec] + [pl.BlockSpec(memory_space=pl.ANY)] * len(stacked),
        out_specs=[t_blk, t_blk, pl.BlockSpec((1, S, LANES), lambda b, hp: (b, 0, hp))],
        out_shape=[jax.ShapeDtypeStruct((DEPTH, B, H_A, HEAD_DIM, S), F32),
                   jax.ShapeDtypeStruct((DEPTH, B, H_A, HEAD_DIM, S), F32),
                   jax.ShapeDtypeStruct((B, S, MIX_A), BF16)],
        input_output_aliases={n_in + i: i for i in range(len(stacked))},
        scratch_shapes=([pltpu.VMEM((S, LANES), F32)] * 2 + [pltpu.VMEM((n_g - 1, 3, S, LANES), F32)]
                        + [pltpu.VMEM((n_g, S, LANES), F32)] * 3),
        compiler_params=_cparams("parallel", "parallel"),
        name="attn_prompt",
    )(proj, proj, proj, bias_tiles, qg, kg, og, *stacked)


def _gla_p_kernel(q_ref, k_ref, v_ref, la_ref, r_ref, g_ref, ob_ref, st_ref, st_scr, *, seq):
    C = GLA_BLOCK
    n_sub = C // GLA_SUB
    st_scr[...] = jnp.zeros(st_scr.shape, F32)
    row = lax.broadcasted_iota(jnp.int32, (C, C), 0)
    tri = row >= lax.broadcasted_iota(jnp.int32, (C, C), 1)
    tri_b = jnp.where(tri, 1.0, 0.0).astype(BF16)
    krow = lax.broadcasted_iota(jnp.int32, (C, DK_B), 0)

    heads = range(GLA_PAR)
    ksl = [slice(DK_B * h, DK_B * (h + 1)) for h in heads]
    vsl = [slice(DV_B * h, DV_B * (h + 1)) for h in heads]

    def block(c, carry):
        rs = pl.ds(pl.multiple_of(c * C, C), C)
        la = [la_ref[0, rs, ksl[h]] for h in heads]
        hi = [x.astype(BF16) for x in la]
        r1 = [x - y.astype(F32) for x, y in zip(la, hi)]
        mid = [x.astype(BF16) for x in r1]
        low = [(x - y.astype(F32)).astype(BF16) for x, y in zip(r1, mid)]
        cum = [_dot(tri_b, a) + _dot(tri_b, b) + _dot(tri_b, d) for a, b, d in zip(hi, mid, low)]
        q = [q_ref[0, rs, ksl[h]] for h in heads]
        k = [k_ref[0, rs, ksl[h]] for h in heads]
        v = [v_ref[0, rs, vsl[h]] for h in heads]
        starts = [[(cum[h] - la[h])[GLA_SUB * i:GLA_SUB * i + 1, :] for i in range(n_sub)] for h in heads]
        start_rows = [jnp.concatenate([jnp.broadcast_to(s, (GLA_SUB, DK_B)) for s in starts[h]], axis=0)
                      for h in heads]
        qe = [(q[h] * jnp.exp(cum[h] - start_rows[h])).astype(BF16) for h in heads]
        khat = [[(k[h] * jnp.exp(jnp.where(krow < GLA_SUB * (i + 1), starts[h][i] - cum[h], NEG))).astype(BF16)
                 for i in range(n_sub)] for h in heads]
        att = [jnp.concatenate([_dot_nt(qe[h][GLA_SUB * i:GLA_SUB * (i + 1)], khat[h][i]) for i in range(n_sub)],
                               axis=0) for h in heads]
        att = [jnp.where(tri, x, 0.0).astype(BF16) for x in att]
        o = [_dot(att[h], v[h].astype(BF16))
             + _dot_nt((q[h] * jnp.exp(cum[h])).astype(BF16), st_scr[h].astype(BF16)) for h in heads]
        last = [x[C - 1:C, :] for x in cum]
        kd = [(k[h] * jnp.exp(last[h] - cum[h])).astype(BF16) for h in heads]
        upd = [_dot(v[h].T.astype(BF16), kd[h]) for h in heads]
        for h in heads:
            st_scr[h] = st_scr[h] * jnp.exp(last[h]) + upd[h]
            ob_ref[0, rs, vsl[h]] = (_rms(o[h], g_ref[...]) * _silu(r_ref[0, rs, vsl[h]])).astype(BF16)
        return carry

    lax.fori_loop(0, seq // C, block, 0, unroll=GLA_UNROLL)
    for h in heads:
        st_ref[0, h] = st_scr[h].T


def _gla_prompt(proj, log_a, gla_g):
    B, S, _ = proj.shape
    kw, vw = GLA_PAR * DK_B, GLA_PAR * DV_B
    q0 = 3 * MIX_A // kw
    k0 = q0 + KEY_B // kw
    v0 = (3 * MIX_A + 2 * KEY_B) // vw
    r0 = v0 + MIX_B // vw
    kblk = lambda off: pl.BlockSpec((1, S, kw), lambda b, h: (b, 0, off + h))
    vblk = lambda off: pl.BlockSpec((1, S, vw), lambda b, h: (b, 0, off + h))
    return pl.pallas_call(
        functools.partial(_gla_p_kernel, seq=S),
        grid=(B, H_B // GLA_PAR),
        in_specs=[kblk(q0), kblk(k0), vblk(v0), kblk(0), vblk(r0),
                  pl.BlockSpec((1, DV_B), lambda b, h: (0, 0))],
        out_specs=[vblk(0), pl.BlockSpec((1, GLA_PAR, DK_B, DV_B), lambda b, h: (b, h, 0, 0))],
        out_shape=[jax.ShapeDtypeStruct((B, S, MIX_B), BF16),
                   jax.ShapeDtypeStruct((B, H_B, DK_B, DV_B), F32)],
        scratch_shapes=[pltpu.VMEM((GLA_PAR, DV_B, DK_B), F32)],
        compiler_params=_cparams("parallel", "parallel"),
        name="gla_prompt",
    )(proj, proj, proj, log_a, proj, gla_g)


def _gla_s_kernel(q_ref, k_ref, v_ref, la_ref, r_ref, g_ref, s_ref, ob_ref, so_ref, *, steps):
    rowi = lax.broadcasted_iota(jnp.int32, (DK_B, DK_B), 0)
    heads = range(H_B)
    ks = [slice(DK_B * h, DK_B * (h + 1)) for h in heads]
    vs = [slice(DV_B * h, DV_B * (h + 1)) for h in heads]
    x = [jnp.zeros((DK_B, DK_B), F32) for _ in heads]
    for t in range(steps):
        x = [jnp.where(rowi == t, k_ref[0, t:t + 1, ks[h]], x[h]) for h in heads]
        x = [jnp.where(rowi == steps + t, jnp.exp(la_ref[0, t:t + 1, ks[h]]), x[h]) for h in heads]
        x = [jnp.where(rowi == 2 * steps + t, q_ref[0, t:t + 1, ks[h]], x[h]) for h in heads]
    xt = [a.T for a in x]
    st = [s_ref[0, h] for h in heads]
    for t in range(steps):
        st = [xt[h][:, steps + t:steps + t + 1] * st[h] + xt[h][:, t:t + 1] * v_ref[0, t:t + 1, vs[h]]
              for h in heads]
        o = [jnp.sum(xt[h][:, 2 * steps + t:2 * steps + t + 1] * st[h], axis=0, keepdims=True) for h in heads]
        for h in heads:
            ob_ref[0, t:t + 1, vs[h]] = _rms(o[h], g_ref[...]) * _silu(r_ref[0, t:t + 1, vs[h]])
    for h in heads:
        so_ref[0, h] = st[h]


def _gla_sample(l, q, k, v, log_a, r, gla_g, state):
    B, T, _ = q.shape
    kb = pl.BlockSpec((1, T, KEY_B), lambda b: (b, 0, 0))
    vb = pl.BlockSpec((1, T, MIX_B), lambda b: (b, 0, 0))
    return pl.pallas_call(
        functools.partial(_gla_s_kernel, steps=T),
        grid=(B,),
        in_specs=[kb, kb, vb, kb, vb, pl.BlockSpec((1, DV_B), lambda b: (0, 0)),
                  pl.BlockSpec((None, 1, H_B, DK_B, DV_B), lambda b: (l, b, 0, 0, 0))],
        out_specs=[vb, pl.BlockSpec((1, H_B, DK_B, DV_B), lambda b: (b, 0, 0, 0))],
        out_shape=[jax.ShapeDtypeStruct((B, T, MIX_B), F32),
                   jax.ShapeDtypeStruct((B, H_B, DK_B, DV_B), F32)],
        compiler_params=_cparams("parallel"),
        name="gla_sample",
    )(q, k, v, log_a, r, gla_g, state)


def _attn_s_kernel(q_ref, k_ref, v_ref, kt_ref, vt_ref, bnear_ref, bfar_ref, bnew_ref,
                   qg_ref, kg_ref, og_ref, kn_ref, oa_ref, kpad_scr, vpad_scr, *, n_near):
    n_head, n_row, _ = kn_ref.shape[1:]
    width = kt_ref.shape[-1]
    kn = _rms(k_ref[0], kg_ref[...])
    kn_ref[0] = kn
    qs = _rms(q_ref[0], qg_ref[...]) * ATTN_SCALE
    kpad_scr[...] = jnp.zeros(kpad_scr.shape, F32)
    vpad_scr[...] = jnp.zeros(vpad_scr.shape, F32)
    kpad_scr[:, 0:n_row, :] = kn
    vpad_scr[:, 0:n_row, :] = v_ref[0]
    n_g = len(DIL_GROUPS)
    for h in range(n_head):
        qh = qs[h].astype(BF16)
        s = _dot(qh, kt_ref[h].astype(BF16))
        s_new = _dot_nt(qh, kpad_scr[h].astype(BF16))
        far = s + bfar_ref[h]
        s_near = s[:, width - n_near:]
        near = [s_near + bnear_ref[g, h] for g in range(n_g - 1)]
        new = [s_new + bnew_ref[g, h] for g in range(n_g)]
        m = jnp.max(far, axis=-1, keepdims=True)
        for x in near + new:
            m = jnp.maximum(m, jnp.max(x, axis=-1, keepdims=True))
        p_far = jnp.exp(far - m)
        p_near = sum(jnp.exp(x - m) for x in near)
        p_new = sum(jnp.exp(x - m) for x in new)
        den = (jnp.sum(p_far, axis=-1, keepdims=True) + jnp.sum(p_near, axis=-1, keepdims=True)
               + jnp.sum(p_new, axis=-1, keepdims=True))
        p = jnp.concatenate([p_far[:, :width - n_near], p_far[:, width - n_near:] + p_near], axis=1)
        o = _dot_nt(p.astype(BF16), vt_ref[h].astype(BF16)) + _dot(p_new.astype(BF16), vpad_scr[h].astype(BF16))
        oa_ref[0, h] = _rms(o / den, og_ref[...])


def _attn_sample(l, qkv, cache_kt, cache_vt, bias_near, bias_far, bias_new, qg, kg, og):
    _, B, n_head, n_row, _ = qkv.shape
    W = cache_kt.shape[-1]
    n_near = bias_near.shape[-1]
    qblk = lambda i: pl.BlockSpec((None, 1, n_head, n_row, HEAD_DIM), lambda b: (i, b, 0, 0, 0))
    cblk = pl.BlockSpec((None, None, n_head, HEAD_DIM, W), lambda b: (l, b, 0, 0, 0))
    full = lambda a: pl.BlockSpec(a.shape, lambda b: (0,) * a.ndim)
    vec = pl.BlockSpec((1, HEAD_DIM), lambda b: (0, 0))
    out_blk = pl.BlockSpec((1, n_head, n_row, HEAD_DIM), lambda b: (b, 0, 0, 0))
    return pl.pallas_call(
        functools.partial(_attn_s_kernel, n_near=n_near),
        grid=(B,),
        in_specs=[qblk(0), qblk(1), qblk(2), cblk, cblk,
                  full(bias_near), full(bias_far), full(bias_new), vec, vec, vec],
        out_specs=[out_blk, out_blk],
        out_shape=[jax.ShapeDtypeStruct((B, n_head, n_row, HEAD_DIM), F32),
                   jax.ShapeDtypeStruct((B, n_head, n_row, HEAD_DIM), F32)],
        scratch_shapes=[pltpu.VMEM((n_head, BAND, HEAD_DIM), F32)] * 2,
        compiler_params=_cparams("parallel"),
        name="attn_sample",
    )(qkv, qkv, qkv, cache_kt, cache_vt, bias_near, bias_far, bias_new, qg, kg, og)


def _bucket_table(n):
    dist = np.arange(n)
    exact = N_BUCKETS // 2
    df = np.maximum(dist, 1).astype(np.float32)
    large = exact + (np.log(df / np.float32(exact)) / np.float32(math.log(MAX_DISTANCE / exact))
                     * np.float32(N_BUCKETS - exact)).astype(np.int32)
    return np.where(dist < exact, dist, np.minimum(large, N_BUCKETS - 1))


def _prompt_bias_tiles(rel_bias):
    span = 3 * BAND
    tiles = []
    for _, dil in DIL_GROUPS:
        steps = np.arange(BAND, -1, -1)
        vec = rel_bias[_bucket_table(BAND * dil + 1)[steps * dil]].T.astype(F32)
        u = jnp.concatenate([vec, jnp.full((H_A, span - BAND - 1), NEG, F32)], axis=1)
        t = jnp.tile(u, (1, BAND))[:, :BAND * (span - 1)].reshape(H_A, BAND, span - 1)
        tiles.append(t[:, :, :2 * BAND])
    return jnp.stack(tiles)


def _sample_bias_tiles(rel_bias, steps, rows, wbuf):
    n_dist = wbuf + rows
    bucket = _bucket_table(n_dist)
    dist = np.arange(n_dist)
    tab = rel_bias[bucket].T.astype(F32)
    n_near = DIL_GROUPS[-2][0]
    tiles = []
    for win, dil in DIL_GROUPS:
        ok = (dist % dil == 0) & (dist >= dil) & (dist <= win)
        rev = jnp.where(ok[None, ::-1], tab[:, ::-1], NEG)
        tiles.append(jnp.stack([rev[:, rows - 1 - t:rows - 1 - t + wbuf] for t in range(rows)], axis=1))
    near = jnp.stack([t[:, :, wbuf - n_near:] for t in tiles[:-1]])
    far = tiles[-1]
    t_new = np.arange(rows)[:, None]
    tau = np.arange(BAND)[None, :]
    live = (t_new < steps) & (tau < steps)
    back = np.clip(t_new - tau, 0, steps)
    new = []
    for g, (_, dil) in enumerate(DIL_GROUPS):
        ok = live & ((tau <= t_new) if g == 0 else (tau == t_new))
        new.append(jnp.where(ok[None], rel_bias[bucket[back * dil]].transpose(2, 0, 1).astype(F32), NEG))
    return near, far, jnp.stack(new)


def kernel(x_prompt, x_sample, cache_k_win, cache_v_win, state_gla, state_conv, c_prompt, c_sample,
           rel_bias, norm1_g, norm2_g, w_ada, b_ada, w_in, q_norm_g, k_norm_g, w_gate2, b_gate2,
           a_out_g, gla_out_g, w_out, w_up, conv_w, conv_b, w_down):
    B, S, _ = x_prompt.shape
    Bs, T, _ = x_sample.shape
    wbuf = cache_k_win.shape[2]
    Mp, Ms = B * S, Bs * T
    assert wbuf == DIL_GROUPS[-1][0] and T <= SUBLANES

    w_main = _main_cols_bf16(w_in)
    w_glr = jnp.pad(w_in[:, :, MAIN_COLS:], ((0, 0), (0, 0), (0, LANES - GATE_RANK))).astype(BF16)
    w_g2 = jnp.pad(w_gate2, ((0, 0), (0, LANES - GATE_RANK), (0, 0))).astype(BF16)
    b_g2 = b_gate2.reshape(DEPTH, 1, KEY_B)
    w_out_b = w_out.astype(BF16)
    w_up_b = w_up.astype(BF16)
    w_down_b = w_down.astype(BF16)
    col_scale = jnp.ones((1, MAIN_COLS), F32).at[:, 3 * MIX_A:3 * MIX_A + KEY_B].set(DK_B ** -0.5)
    vec3 = lambda a: a.reshape(DEPTH, 1, -1)
    pair = lambda a, l: jnp.tile(a[l], LANES // HEAD_DIM).reshape(1, LANES)
    head = lambda a, l: a[l].reshape(1, HEAD_DIM)

    mods = _ada_mods(jnp.concatenate([c_prompt, c_sample], axis=0), w_ada, b_ada)

    bias_p = _prompt_bias_tiles(rel_bias)
    bias_near, bias_far, bias_new = _sample_bias_tiles(rel_bias, T, SUBLANES, wbuf)

    cache_kt = jnp.transpose(cache_k_win, (0, 1, 3, 4, 2))
    cache_vt = jnp.transpose(cache_v_win, (0, 1, 3, 4, 2))
    hist_p = jnp.zeros((B, CONV_W - 1, 1, D_FF), F32)

    xp = x_prompt.reshape(Mp, D_MODEL)
    xs = jnp.transpose(x_sample, (1, 0, 2)).reshape(Ms, D_MODEL)
    outs = [[] for _ in range(6)]
    kv_p = ()
    for l in range(DEPTH):
        mod_p = [m.reshape(B, 1, D_MODEL) for m in jnp.split(mods[l, :B], 6, axis=-1)]
        mod_s = [jnp.tile(m, (T, 1)).reshape(1, Ms, D_MODEL) for m in jnp.split(mods[l, B:], 6, axis=-1)]

        proj, log_a = _in_proj(l, xp, vec3(norm1_g), mod_p[1], mod_p[0], w_main, col_scale, w_glr, w_g2, b_g2,
                               tm=TM_IN, rows_per_group=S)
        proj = proj.reshape(B, S, MAIN_COLS)
        *kv_p, oa = _attn_prompt(l, proj, bias_p, pair(q_norm_g, l), pair(k_norm_g, l), pair(a_out_g, l), kv_p)
        ob, g_st = _gla_prompt(proj, log_a.reshape(B, S, KEY_B), gla_out_g[l].reshape(1, DV_B))
        xp, h2 = _out_proj(l, oa.reshape(Mp, MIX_A), ob.reshape(Mp, MIX_B), xp, mod_p[2], w_out_b,
                           vec3(norm2_g), mod_p[4], mod_p[3], tm=TM_OUT, rows_per_group=S)
        act, tail = _up_proj(l, h2, w_up_b, conv_w, vec3(conv_b), hist_p, n_seq=1, steps=S, tn=TN_UP)
        xp = _down_proj(l, act, w_down_b, xp, mod_p[5], tm=TM_DOWN, tn=TN_DOWN, rows_per_group=S)
        outs[0].append(g_st)
        outs[1].append(tail[:, tail.shape[1] - (CONV_W - 1):])

        proj, log_a = _in_proj(l, xs, vec3(norm1_g), mod_s[1], mod_s[0], w_main, col_scale, w_glr, w_g2, b_g2,
                               tm=Ms, rows_per_group=Ms)
        bm = lambda a: jnp.transpose(a.reshape(T, Bs, -1), (1, 0, 2))
        qkv = jnp.transpose(proj[:, :3 * MIX_A].reshape(T, Bs, 3, H_A, HEAD_DIM), (2, 1, 3, 0, 4))
        qkv_pad = jnp.pad(qkv, ((0, 0), (0, 0), (0, 0), (0, SUBLANES - T), (0, 0)))
        kn_s, oa_s = _attn_sample(l, qkv_pad, cache_kt, cache_vt, bias_near, bias_far, bias_new,
                                  head(q_norm_g, l), head(k_norm_g, l), head(a_out_g, l))
        gla_in = bm(proj[:, 3 * MIX_A:])
        ob_s, g_st = _gla_sample(l, gla_in[..., :KEY_B], gla_in[..., KEY_B:2 * KEY_B],
                                 gla_in[..., 2 * KEY_B:2 * KEY_B + MIX_B], bm(log_a),
                                 gla_in[..., 2 * KEY_B + MIX_B:], gla_out_g[l].reshape(1, DV_B), state_gla)
        oa_t = jnp.transpose(oa_s[:, :, :T], (2, 0, 1, 3)).reshape(Ms, MIX_A).astype(BF16)
        ob_t = jnp.transpose(ob_s, (1, 0, 2)).reshape(Ms, MIX_B).astype(BF16)
        xs, h2 = _out_proj(l, oa_t, ob_t, xs, mod_s[2], w_out_b,
                           vec3(norm2_g), mod_s[4], mod_s[3], tm=Ms, rows_per_group=Ms)
        hist_s = jnp.transpose(state_conv[l], (1, 0, 2)).reshape(1, CONV_W - 1, Bs, D_FF)
        act, tail = _up_proj(l, h2, w_up_b, conv_w, vec3(conv_b), hist_s, n_seq=Bs, steps=T, tn=TN_UP)
        xs = _down_proj(l, act, w_down_b, xs, mod_s[5], tm=Ms, tn=TN_DOWN, rows_per_group=Ms)
        outs[2].append(jnp.transpose(kn_s[:, :, :T], (0, 2, 1, 3)))
        outs[3].append(jnp.transpose(qkv[2], (0, 2, 1, 3)))
        outs[4].append(g_st)
        outs[5].append(jnp.transpose(tail.reshape(CONV_W - 1, Bs, D_FF), (1, 0, 2)))

    y_prompt = xp.reshape(B, S, D_MODEL)
    y_sample = jnp.transpose(xs.reshape(T, Bs, D_MODEL), (1, 0, 2))
    kv_prompt = tuple(jnp.transpose(a, (0, 1, 4, 2, 3)) for a in kv_p)
    return (y_prompt, y_sample) + kv_prompt + tuple(jnp.stack(o) for o in outs)
```

```python
import functools
import math

import numpy as np
import jax
import jax.numpy as jnp
from jax import lax
from jax.experimental import pallas as pl
from jax.experimental.pallas import tpu as pltpu

F32 = jnp.float32
BF16 = jnp.bfloat16

D_MODEL = 2048
DEPTH = 2
HEAD_DIM = 64
MIX_A = D_MODEL // 2
H_A = MIX_A // HEAD_DIM
DIL_GROUPS = ((128, 1), (512, 4), (2048, 16))
N_BUCKETS = 32
MAX_DISTANCE = 2048
MIX_B = D_MODEL - MIX_A
H_B = 4
DV_B = MIX_B // H_B
DK_B = DV_B // 2
KEY_B = H_B * DK_B
GATE_RANK = 16
GATE_TAU = 16.0
D_FF = ((8 * D_MODEL // 3 + 255) // 256) * 256
CONV_W = 3
EPS = 1e-6
ATTN_SCALE = HEAD_DIM ** -0.5
MAIN_COLS = 3 * MIX_A + 2 * KEY_B + 2 * MIX_B
BAND = 128
NEG = -1e30

LANES = 128
SUBLANES = 8
VMEM_LIMIT = 56 * 1024 * 1024
TM_IN, TM_OUT, TM_DOWN = 1024, 512, 1024
TN_UP, TN_DOWN = 512, 512
UP_SPLIT, OUT_SPLIT = 2, 2
GLA_BLOCK = 128
GLA_SUB = 16
ATTN_PAR = 8
GLA_UNROLL = 4
GLA_PAR = 2

_NT = (((1,), (1,)), ((), ()))


def _cparams(*sem):
    return pltpu.CompilerParams(dimension_semantics=sem, vmem_limit_bytes=VMEM_LIMIT)


def _dot(a, b):
    return jnp.dot(a, b, preferred_element_type=F32)


def _dot_nt(a, b):
    return lax.dot_general(a, b, _NT, preferred_element_type=F32)


def _silu(x):
    return x * jax.nn.sigmoid(x)


def _rms(x, g):
    return x * lax.rsqrt(jnp.mean(x * x, axis=-1, keepdims=True) + EPS) * g


def _ada_kernel(c_ref, w_ref, b_ref, o_ref):
    a = _silu(c_ref[...]).astype(BF16)
    o_ref[...] = _dot(a, w_ref[...].astype(BF16)) + b_ref[...]


def _ada_mods(c_all, w_ada, b_ada):
    G = c_all.shape[0]
    tn = 1024
    return pl.pallas_call(
        _ada_kernel,
        grid=(DEPTH, 6 * D_MODEL // tn),
        in_specs=[pl.BlockSpec((G, D_MODEL), lambda l, j: (0, 0)),
                  pl.BlockSpec((None, D_MODEL, tn), lambda l, j: (l, 0, j)),
                  pl.BlockSpec((None, 1, tn), lambda l, j: (l, 0, j))],
        out_specs=pl.BlockSpec((None, G, tn), lambda l, j: (l, 0, j)),
        out_shape=jax.ShapeDtypeStruct((DEPTH, G, 6 * D_MODEL), F32),
        compiler_params=_cparams("parallel", "parallel"),
        name="ada_mods",
    )(c_all, w_ada, b_ada.reshape(DEPTH, 1, 6 * D_MODEL))


def _cast_kernel(x_ref, o_ref):
    o_ref[...] = x_ref[...].astype(BF16)


def _main_cols_bf16(w_in_t):
    tn = 1024
    return pl.pallas_call(
        _cast_kernel,
        grid=(DEPTH, MAIN_COLS // tn),
        in_specs=[pl.BlockSpec((None, tn, D_MODEL), lambda l, j: (l, j, 0))],
        out_specs=pl.BlockSpec((None, tn, D_MODEL), lambda l, j: (l, j, 0)),
        out_shape=jax.ShapeDtypeStruct((DEPTH, MAIN_COLS, D_MODEL), BF16),
        compiler_params=_cparams("parallel", "parallel"),
        name="w_in_bf16",
    )(w_in_t)


def _in_kernel(x_ref, g_ref, sc_ref, sh_ref, w_ref, cs_ref, wg_ref, wg2_ref, bg2_ref,
               o_ref, la_ref, h_scr):
    @pl.when(pl.program_id(1) == 0)
    def _():
        hb = (_rms(x_ref[...], g_ref[...]) * (1.0 + sc_ref[...]) + sh_ref[...]).astype(BF16)
        h_scr[...] = hb
        glr = _dot_nt(hb, wg_ref[...])
        z = _dot(glr.astype(BF16), wg2_ref[...]) + bg2_ref[...]
        la_ref[...] = (jnp.minimum(z, 0.0) - jnp.log(1.0 + jnp.exp(-jnp.abs(z)))) * (1.0 / GATE_TAU)

    o_ref[...] = _dot_nt(h_scr[...], w_ref[...]) * cs_ref[...]


def _in_proj(l, x, norm_g, scale, shift, w_main, col_scale, w_glr, w_g2, b_g2, tm, rows_per_group):
    M = x.shape[0]
    tn = 1024
    R = scale.shape[1]
    grp = lambda i, j: ((i * tm) // rows_per_group, 0, 0)
    return pl.pallas_call(
        _in_kernel,
        grid=(M // tm, MAIN_COLS // tn),
        in_specs=[pl.BlockSpec((tm, D_MODEL), lambda i, j: (i, 0)),
                  pl.BlockSpec((None, 1, D_MODEL), lambda i, j: (l, 0, 0)),
                  pl.BlockSpec((None, R, D_MODEL), grp),
                  pl.BlockSpec((None, R, D_MODEL), grp),
                  pl.BlockSpec((None, tn, D_MODEL), lambda i, j: (l, j, 0)),
                  pl.BlockSpec((1, tn), lambda i, j: (0, j)),
                  pl.BlockSpec((None, LANES, D_MODEL), lambda i, j: (l, 0, 0)),
                  pl.BlockSpec((None, LANES, KEY_B), lambda i, j: (l, 0, 0)),
                  pl.BlockSpec((None, 1, KEY_B), lambda i, j: (l, 0, 0))],
        out_specs=[pl.BlockSpec((tm, tn), lambda i, j: (i, j)),
                   pl.BlockSpec((tm, KEY_B), lambda i, j: (i, 0))],
        out_shape=[jax.ShapeDtypeStruct((M, MAIN_COLS), F32),
                   jax.ShapeDtypeStruct((M, KEY_B), F32)],
        scratch_shapes=[pltpu.VMEM((tm, D_MODEL), BF16)],
        compiler_params=_cparams("parallel", "arbitrary"),
        name="in_proj",
    )(x, norm_g, scale, shift, w_main, col_scale, w_glr, w_g2, b_g2)


def _out_kernel(oa_ref, ob_ref, x_ref, g1_ref, wa_ref, wb_ref, n2_ref, sc_ref, sh_ref, xo_ref, h2_ref):
    tm = x_ref.shape[0]
    halves = [pl.ds(c * (tm // OUT_SPLIT), tm // OUT_SPLIT) for c in range(OUT_SPLIT)]
    ys = [_dot(oa_ref[rs, :], wa_ref[...]) + _dot(ob_ref[rs, :], wb_ref[...]) for rs in halves]
    per_row = g1_ref.shape[0] == tm
    for rs, y in zip(halves, ys):
        mod = lambda ref: ref[rs, :] if per_row else ref[...]
        xn = x_ref[rs, :] + mod(g1_ref) * y
        xo_ref[rs, :] = xn
        h2_ref[rs, :] = (_rms(xn, n2_ref[...]) * (1.0 + mod(sc_ref)) + mod(sh_ref)).astype(BF16)


def _out_proj(l, oa, ob, x, gate1, w_out, norm2_g, scale2, shift2, tm, rows_per_group):
    M = x.shape[0]
    R = gate1.shape[1]
    grp = lambda i: ((i * tm) // rows_per_group, 0, 0)
    return pl.pallas_call(
        _out_kernel,
        grid=(M // tm,),
        in_specs=[pl.BlockSpec((tm, MIX_A), lambda i: (i, 0)),
                  pl.BlockSpec((tm, MIX_B), lambda i: (i, 0)),
                  pl.BlockSpec((tm, D_MODEL), lambda i: (i, 0)),
                  pl.BlockSpec((None, R, D_MODEL), grp),
                  pl.BlockSpec((None, MIX_A, D_MODEL), lambda i: (l, 0, 0)),
                  pl.BlockSpec((None, MIX_B, D_MODEL), lambda i: (l, 1, 0)),
                  pl.BlockSpec((None, 1, D_MODEL), lambda i: (l, 0, 0)),
                  pl.BlockSpec((None, R, D_MODEL), grp),
                  pl.BlockSpec((None, R, D_MODEL), grp)],
        out_specs=[pl.BlockSpec((tm, D_MODEL), lambda i: (i, 0)),
                   pl.BlockSpec((tm, D_MODEL), lambda i: (i, 0))],
        out_shape=[jax.ShapeDtypeStruct((M, D_MODEL), F32),
                   jax.ShapeDtypeStruct((M, D_MODEL), BF16)],
        compiler_params=_cparams("parallel"),
        name="out_proj",
    )(oa, ob, x, gate1, w_out, w_out, norm2_g, scale2, shift2)


def _up_kernel(h_ref, wg_ref, wu_ref, cw_ref, cb_ref, hist_ref, act_ref, tail_ref, *, n_seq, steps):
    h = h_ref[...]
    rows = n_seq * steps
    tail = tail_ref.shape[0]
    tn = act_ref.shape[1]
    halves = [slice(c * (tn // UP_SPLIT), (c + 1) * (tn // UP_SPLIT)) for c in range(UP_SPLIT)]
    gs = [_dot(h, wg_ref[:, cs]) for cs in halves]
    gates = []
    for cs, g in zip(halves, gs):
        row = lax.broadcasted_iota(jnp.int32, g.shape, 0)

        def per_row(hist, g=g):
            if n_seq == 1:
                return jnp.broadcast_to(hist, g.shape)
            return jnp.concatenate([hist] * steps, axis=0)

        h0 = per_row(hist_ref[0, :, cs])
        h1 = per_row(hist_ref[1, :, cs])
        g1 = jnp.where(row < n_seq, h1, pltpu.roll(g, n_seq, 0))
        g2 = jnp.where(row < n_seq, h0, jnp.where(row < 2 * n_seq, h1, pltpu.roll(g, 2 * n_seq, 0)))
        cw = cw_ref[:, cs]
        conv = cb_ref[:, cs] + cw[0:1] * g2 + cw[1:2] * g1 + cw[2:3] * g
        gates.append(_silu(conv))
        tail_ref[:, cs] = g[rows - tail:]
    for cs, gate in zip(halves, gates):
        act_ref[:, cs] = (gate * _dot(h, wu_ref[:, cs])).astype(BF16)


def _up_proj(l, h2, w_up, conv_w, conv_b, hist, n_seq, steps, tn):
    M = h2.shape[0]
    tm = n_seq * steps
    nj = D_FF // tn
    tail = max(SUBLANES, (CONV_W - 1) * n_seq)
    return pl.pallas_call(
        functools.partial(_up_kernel, n_seq=n_seq, steps=steps),
        grid=(M // tm, nj),
        in_specs=[pl.BlockSpec((tm, D_MODEL), lambda i, j: (i, 0)),
                  pl.BlockSpec((None, D_MODEL, tn), lambda i, j: (l, 0, j)),
                  pl.BlockSpec((None, D_MODEL, tn), lambda i, j: (l, 0, nj + j)),
                  pl.BlockSpec((None, CONV_W, tn), lambda i, j: (l, 0, j)),
                  pl.BlockSpec((None, 1, tn), lambda i, j: (l, 0, j)),
                  pl.BlockSpec((None, CONV_W - 1, n_seq, tn), lambda i, j: (i, 0, 0, j))],
        out_specs=[pl.BlockSpec((tm, tn), lambda i, j: (i, j)),
                   pl.BlockSpec((None, tail, tn), lambda i, j: (i, 0, j))],
        out_shape=[jax.ShapeDtypeStruct((M, D_FF), BF16),
                   jax.ShapeDtypeStruct((M // tm, tail, D_FF), F32)],
        compiler_params=_cparams("parallel", "parallel"),
        name="up_proj",
    )(h2, w_up, w_up, conv_w, conv_b, hist)


def _down_kernel(a_ref, w_ref, x_ref, g2_ref, o_ref):
    o_ref[...] = x_ref[...] + g2_ref[...] * _dot(a_ref[...], w_ref[...])


def _down_proj(l, act, w_down, x, gate2, tm, tn, rows_per_group):
    M = x.shape[0]
    R = gate2.shape[1]
    return pl.pallas_call(
        _down_kernel,
        grid=(M // tm, D_MODEL // tn),
        in_specs=[pl.BlockSpec((tm, D_FF), lambda i, j: (i, 0)),
                  pl.BlockSpec((None, D_FF, tn), lambda i, j: (l, 0, j)),
                  pl.BlockSpec((tm, tn), lambda i, j: (i, j)),
                  pl.BlockSpec((None, R, tn), lambda i, j: ((i * tm) // rows_per_group, 0, j))],
        out_specs=pl.BlockSpec((tm, tn), lambda i, j: (i, j)),
        out_shape=jax.ShapeDtypeStruct((M, D_MODEL), F32),
        compiler_params=_cparams("parallel", "parallel"),
        name="down_proj",
    )(act, w_down, x, gate2)


def _head_pair_norm(x, g, same_head):
    x2 = x * x
    hi = x2.astype(BF16)
    low = (x2 - hi.astype(F32)).astype(BF16)
    ms = (_dot(hi, same_head) + _dot(low, same_head)) * (1.0 / HEAD_DIM)
    return x * lax.rsqrt(ms + EPS) * g


def _attn_p_kernel(q_ref, k_ref, v_ref, bias_ref, qg_ref, kg_ref, og_ref, *rest, seq, n_alias):
    knt_ref, vt_ref, oa_ref, qs_scr, ks_scr, perm_scr, m_scr, l_scr, acc_scr = rest[n_alias:]
    lo = lax.broadcasted_iota(jnp.int32, (1, LANES), 1) < HEAD_DIM
    head_of = lambda axis: lax.broadcasted_iota(jnp.int32, (LANES, LANES), axis) // HEAD_DIM
    same_head = jnp.where(head_of(0) == head_of(1), 1.0, 0.0).astype(BF16)
    kn = _head_pair_norm(k_ref[0], kg_ref[...], same_head)
    ks_scr[...] = kn
    kn_t = kn.T
    for ref, val_t in ((knt_ref, kn_t), (vt_ref, v_ref[0].T)):
        if n_alias:
            ref[0] = val_t.reshape(2, HEAD_DIM, seq)
        else:
            ref[0, 0] = val_t.reshape(2, HEAD_DIM, seq)
            ref[1:, 0] = jnp.zeros((ref.shape[0] - 1, 2, HEAD_DIM, seq), F32)
    qs_scr[...] = _head_pair_norm(q_ref[0], qg_ref[...], same_head) * ATTN_SCALE
    col = lax.broadcasted_iota(jnp.int32, (BAND, 2 * BAND), 1)

    stage_rows = [(lambda rs: qs_scr[rs, :], lambda rs: ks_scr[rs, :], lambda rs: v_ref[0, rs, :])]
    for g in range(1, len(DIL_GROUPS)):
        d_prev, d_cur = DIL_GROUPS[g - 1][1], DIL_GROUPS[g][1]
        ratio = d_cur // d_prev
        len_prev, len_cur = seq // d_prev, seq // d_cur
        for t in range(3):
            for c in range(d_prev):
                for m in range(ratio):
                    src = pl.ds(c * len_prev + m, len_cur, stride=ratio)
                    perm_scr[g - 1, t, pl.ds((c + d_prev * m) * len_cur, len_cur), :] = stage_rows[g - 1][t](src)
        stage_rows.append(tuple((lambda rs, g=g, t=t: perm_scr[g - 1, t, rs, :]) for t in range(3)))

    for g, (_, dil) in enumerate(DIL_GROUPS):
        n_blk = seq // dil // BAND
        has_prev = n_blk > 1
        q_rows, k_rows, v_rows = stage_rows[g]

        def rows(start, dil=dil):
            return pl.ds(start, BAND, stride=dil) if dil > 1 else pl.ds(start, BAND)

        def body(trip, carry, g=g, dil=dil, n_blk=n_blk, rows=rows, has_prev=has_prev,
                 q_rows=q_rows, k_rows=k_rows, v_rows=v_rows):
            blocks = []
            for u in range(ATTN_PAR):
                idx = trip * ATTN_PAR + u
                r = idx // n_blk
                nb = idx % n_blk
                cur = pl.ds(pl.multiple_of((r * n_blk + nb) * BAND, BAND), BAND)
                qb = q_rows(cur)
                kcat = k_rows(cur)
                vcat = v_rows(cur)
                bias = [bias_ref[g, h, :, BAND:] for h in range(2)]
                if has_prev:
                    prev = pl.ds(pl.multiple_of((r * n_blk + jnp.maximum(nb - 1, 0)) * BAND, BAND), BAND)
                    kcat = jnp.concatenate([k_rows(prev), kcat], axis=0)
                    vcat = jnp.concatenate([v_rows(prev), vcat], axis=0)
                    no_prev = jnp.logical_and(nb == 0, col < BAND)
                    bias = [jnp.where(no_prev, NEG, bias_ref[g, h]) for h in range(2)]
                blocks.append(dict(out=rows(r + nb * (BAND * dil)), q=qb, k=kcat.astype(BF16),
                                   v=vcat.astype(BF16), bias=bias))
            heads = [(blk, h) for blk in blocks for h in range(2)]
            s = [_dot_nt(jnp.where(lo if h == 0 else jnp.logical_not(lo), blk["q"], 0.0).astype(BF16), blk["k"])
                 + blk["bias"][h] for blk, h in heads]
            mx = [jnp.max(x, axis=-1, keepdims=True) for x in s]
            p = [jnp.exp(x - m) for x, m in zip(s, mx)]
            sm = [jnp.sum(x, axis=-1, keepdims=True) for x in p]
            o = [_dot(x.astype(BF16), blk["v"]) for x, (blk, _) in zip(p, heads)]
            for u, blk in enumerate(blocks):
                m_scr[g, blk["out"], :] = jnp.where(lo, mx[2 * u], mx[2 * u + 1])
                l_scr[g, blk["out"], :] = jnp.where(lo, sm[2 * u], sm[2 * u + 1])
                acc_scr[g, blk["out"], :] = jnp.where(lo, o[2 * u], o[2 * u + 1])
            return carry

        lax.fori_loop(0, seq // BAND // ATTN_PAR, body, 0)

    n_g = len(DIL_GROUPS)
    m_all = m_scr[0]
    for g in range(1, n_g):
        m_all = jnp.maximum(m_all, m_scr[g])
    num = jnp.zeros((seq, LANES), F32)
    den = jnp.zeros((seq, LANES), F32)
    for g in range(n_g):
        w = jnp.exp(m_scr[g] - m_all)
        num = num + w * acc_scr[g]
        den = den + w * l_scr[g]
    oa_ref[0] = _head_pair_norm(num / den, og_ref[...], same_head).astype(BF16)


def _attn_prompt(l, proj, bias_tiles, qg, kg, og, stacked):
    B, S, _ = proj.shape
    assert bool(stacked) == (l > 0)
    n_pair = MIX_A // LANES
    n_g = len(DIL_GROUPS)
    blk = lambda off: pl.BlockSpec((1, S, LANES), lambda b, hp: (b, 0, off + hp))
    vec = pl.BlockSpec((1, LANES), lambda b, hp: (0, 0))
    if stacked:
        t_blk = pl.BlockSpec((None, 1, 2, HEAD_DIM, S), lambda b, hp: (l, b, hp, 0, 0))
    else:
        t_blk = pl.BlockSpec((DEPTH, 1, 2, HEAD_DIM, S), lambda b, hp: (0, b, hp, 0, 0))
    n_in = 7
    return pl.pallas_call(
        functools.partial(_attn_p_kernel, seq=S, n_alias=len(stacked)),
        grid=(B, n_pair),
        in_specs=[blk(0), blk(n_pair), blk(2 * n_pair),
                  pl.BlockSpec((n_g, 2, BAND, 2 * BAND), lambda b, hp: (0, hp, 0, 0)),
                  vec, vec, vec] + [pl.BlockSpec(memory_space=pl.ANY)] * len(stacked),
        out_specs=[t_blk, t_blk, pl.BlockSpec((1, S, LANES), lambda b, hp: (b, 0, hp))],
        out_shape=[jax.ShapeDtypeStruct((DEPTH, B, H_A, HEAD_DIM, S), F32),
                   jax.ShapeDtypeStruct((DEPTH, B, H_A, HEAD_DIM, S), F32),
                   jax.ShapeDtypeStruct((B, S, MIX_A), BF16)],
        input_output_aliases={n_in + i: i for i in range(len(stacked))},
        scratch_shapes=([pltpu.VMEM((S, LANES), F32)] * 2 + [pltpu.VMEM((n_g - 1, 3, S, LANES), F32)]
                        + [pltpu.VMEM((n_g, S, LANES), F32)] * 3),
        compiler_params=_cparams("parallel", "parallel"),
        name="attn_prompt",
    )(proj, proj, proj, bias_tiles, qg, kg, og, *stacked)


def _gla_p_kernel(q_ref, k_ref, v_ref, la_ref, r_ref, g_ref, ob_ref, st_ref, st_scr, *, seq):
    C = GLA_BLOCK
    n_sub = C // GLA_SUB
    st_scr[...] = jnp.zeros(st_scr.shape, F32)
    row = lax.broadcasted_iota(jnp.int32, (C, C), 0)
    tri = row >= lax.broadcasted_iota(jnp.int32, (C, C), 1)
    tri_b = jnp.where(tri, 1.0, 0.0).astype(BF16)
    krow = lax.broadcasted_iota(jnp.int32, (C, DK_B), 0)

    heads = range(GLA_PAR)
    ksl = [slice(DK_B * h, DK_B * (h + 1)) for h in heads]
    vsl = [slice(DV_B * h, DV_B * (h + 1)) for h in heads]

    def block(c, carry):
        rs = pl.ds(pl.multiple_of(c * C, C), C)
        la = [la_ref[0, rs, ksl[h]] for h in heads]
        hi = [x.astype(BF16) for x in la]
        r1 = [x - y.astype(F32) for x, y in zip(la, hi)]
        mid = [x.astype(BF16) for x in r1]
        low = [(x - y.astype(F32)).astype(BF16) for x, y in zip(r1, mid)]
        cum = [_dot(tri_b, a) + _dot(tri_b, b) + _dot(tri_b, d) for a, b, d in zip(hi, mid, low)]
        q = [q_ref[0, rs, ksl[h]] for h in heads]
        k = [k_ref[0, rs, ksl[h]] for h in heads]
        v = [v_ref[0, rs, vsl[h]] for h in heads]
        starts = [[(cum[h] - la[h])[GLA_SUB * i:GLA_SUB * i + 1, :] for i in range(n_sub)] for h in heads]
        start_rows = [jnp.concatenate([jnp.broadcast_to(s, (GLA_SUB, DK_B)) for s in starts[h]], axis=0)
                      for h in heads]
        qe = [(q[h] * jnp.exp(cum[h] - start_rows[h])).astype(BF16) for h in heads]
        khat = [[(k[h] * jnp.exp(jnp.where(krow < GLA_SUB * (i + 1), starts[h][i] - cum[h], NEG))).astype(BF16)
                 for i in range(n_sub)] for h in heads]
        att = [jnp.concatenate([_dot_nt(qe[h][GLA_SUB * i:GLA_SUB * (i + 1)], khat[h][i]) for i in range(n_sub)],
                               axis=0) for h in heads]
        att = [jnp.where(tri, x, 0.0).astype(BF16) for x in att]
        o = [_dot(att[h], v[h].astype(BF16))
             + _dot_nt((q[h] * jnp.exp(cum[h])).astype(BF16), st_scr[h].astype(BF16)) for h in heads]
        last = [x[C - 1:C, :] for x in cum]
        kd = [(k[h] * jnp.exp(last[h] - cum[h])).astype(BF16) for h in heads]
        upd = [_dot(v[h].T.astype(BF16), kd[h]) for h in heads]
        for h in heads:
            st_scr[h] = st_scr[h] * jnp.exp(last[h]) + upd[h]
            ob_ref[0, rs, vsl[h]] = (_rms(o[h], g_ref[...]) * _silu(r_ref[0, rs, vsl[h]])).astype(BF16)
        return carry

    lax.fori_loop(0, seq // C, block, 0, unroll=GLA_UNROLL)
    for h in heads:
        st_ref[0, h] = st_scr[h].T


def _gla_prompt(proj, log_a, gla_g):
    B, S, _ = proj.shape
    kw, vw = GLA_PAR * DK_B, GLA_PAR * DV_B
    q0 = 3 * MIX_A // kw
    k0 = q0 + KEY_B // kw
    v0 = (3 * MIX_A + 2 * KEY_B) // vw
    r0 = v0 + MIX_B // vw
    kblk = lambda off: pl.BlockSpec((1, S, kw), lambda b, h: (b, 0, off + h))
    vblk = lambda off: pl.BlockSpec((1, S, vw), lambda b, h: (b, 0, off + h))
    return pl.pallas_call(
        functools.partial(_gla_p_kernel, seq=S),
        grid=(B, H_B // GLA_PAR),
        in_specs=[kblk(q0), kblk(k0), vblk(v0), kblk(0), vblk(r0),
                  pl.BlockSpec((1, DV_B), lambda b, h: (0, 0))],
        out_specs=[vblk(0), pl.BlockSpec((1, GLA_PAR, DK_B, DV_B), lambda b, h: (b, h, 0, 0))],
        out_shape=[jax.ShapeDtypeStruct((B, S, MIX_B), BF16),
                   jax.ShapeDtypeStruct((B, H_B, DK_B, DV_B), F32)],
        scratch_shapes=[pltpu.VMEM((GLA_PAR, DV_B, DK_B), F32)],
        compiler_params=_cparams("parallel", "parallel"),
        name="gla_prompt",
    )(proj, proj, proj, log_a, proj, gla_g)


def _gla_s_kernel(q_ref, k_ref, v_ref, la_ref, r_ref, g_ref, s_ref, ob_ref, so_ref, *, steps):
    rowi = lax.broadcasted_iota(jnp.int32, (DK_B, DK_B), 0)
    heads = range(H_B)
    ks = [slice(DK_B * h, DK_B * (h + 1)) for h in heads]
    vs = [slice(DV_B * h, DV_B * (h + 1)) for h in heads]
    x = [jnp.zeros((DK_B, DK_B), F32) for _ in heads]
    for t in range(steps):
        x = [jnp.where(rowi == t, k_ref[0, t:t + 1, ks[h]], x[h]) for h in heads]
        x = [jnp.where(rowi == steps + t, jnp.exp(la_ref[0, t:t + 1, ks[h]]), x[h]) for h in heads]
        x = [jnp.where(rowi == 2 * steps + t, q_ref[0, t:t + 1, ks[h]], x[h]) for h in heads]
    xt = [a.T for a in x]
    st = [s_ref[0, h] for h in heads]
    for t in range(steps):
        st = [xt[h][:, steps + t:steps + t + 1] * st[h] + xt[h][:, t:t + 1] * v_ref[0, t:t + 1, vs[h]]
              for h in heads]
        o = [jnp.sum(xt[h][:, 2 * steps + t:2 * steps + t + 1] * st[h], axis=0, keepdims=True) for h in heads]
        for h in heads:
            ob_ref[0, t:t + 1, vs[h]] = _rms(o[h], g_ref[...]) * _silu(r_ref[0, t:t + 1, vs[h]])
    for h in heads:
        so_ref[0, h] = st[h]


def _gla_sample(l, q, k, v, log_a, r, gla_g, state):
    B, T, _ = q.shape
    kb = pl.BlockSpec((1, T, KEY_B), lambda b: (b, 0, 0))
    vb = pl.BlockSpec((1, T, MIX_B), lambda b: (b, 0, 0))
    return pl.pallas_call(
        functools.partial(_gla_s_kernel, steps=T),
        grid=(B,),
        in_specs=[kb, kb, vb, kb, vb, pl.BlockSpec((1, DV_B), lambda b: (0, 0)),
                  pl.BlockSpec((None, 1, H_B, DK_B, DV_B), lambda b: (l, b, 0, 0, 0))],
        out_specs=[vb, pl.BlockSpec((1, H_B, DK_B, DV_B), lambda b: (b, 0, 0, 0))],
        out_shape=[jax.ShapeDtypeStruct((B, T, MIX_B), F32),
                   jax.ShapeDtypeStruct((B, H_B, DK_B, DV_B), F32)],
        compiler_params=_cparams("parallel"),
        name="gla_sample",
    )(q, k, v, log_a, r, gla_g, state)


def _attn_s_kernel(q_ref, k_ref, v_ref, kt_ref, vt_ref, bnear_ref, bfar_ref, bnew_ref,
                   qg_ref, kg_ref, og_ref, kn_ref, oa_ref, kpad_scr, vpad_scr, *, n_near):
    n_head, n_row, _ = kn_ref.shape[1:]
    width = kt_ref.shape[-1]
    kn = _rms(k_ref[0], kg_ref[...])
    kn_ref[0] = kn
    qs = _rms(q_ref[0], qg_ref[...]) * ATTN_SCALE
    kpad_scr[...] = jnp.zeros(kpad_scr.shape, F32)
    vpad_scr[...] = jnp.zeros(vpad_scr.shape, F32)
    kpad_scr[:, 0:n_row, :] = kn
    vpad_scr[:, 0:n_row, :] = v_ref[0]
    n_g = len(DIL_GROUPS)
    for h in range(n_head):
        qh = qs[h].astype(BF16)
        s = _dot(qh, kt_ref[h].astype(BF16))
        s_new = _dot_nt(qh, kpad_scr[h].astype(BF16))
        far = s + bfar_ref[h]
        s_near = s[:, width - n_near:]
        near = [s_near + bnear_ref[g, h] for g in range(n_g - 1)]
        new = [s_new + bnew_ref[g, h] for g in range(n_g)]
        m = jnp.max(far, axis=-1, keepdims=True)
        for x in near + new:
            m = jnp.maximum(m, jnp.max(x, axis=-1, keepdims=True))
        p_far = jnp.exp(far - m)
        p_near = sum(jnp.exp(x - m) for x in near)
        p_new = sum(jnp.exp(x - m) for x in new)
        den = (jnp.sum(p_far, axis=-1, keepdims=True) + jnp.sum(p_near, axis=-1, keepdims=True)
               + jnp.sum(p_new, axis=-1, keepdims=True))
        p = jnp.concatenate([p_far[:, :width - n_near], p_far[:, width - n_near:] + p_near], axis=1)
        o = _dot_nt(p.astype(BF16), vt_ref[h].astype(BF16)) + _dot(p_new.astype(BF16), vpad_scr[h].astype(BF16))
        oa_ref[0, h] = _rms(o / den, og_ref[...])


def _attn_sample(l, qkv, cache_kt, cache_vt, bias_near, bias_far, bias_new, qg, kg, og):
    _, B, n_head, n_row, _ = qkv.shape
    W = cache_kt.shape[-1]
    n_near = bias_near.shape[-1]
    qblk = lambda i: pl.BlockSpec((None, 1, n_head, n_row, HEAD_DIM), lambda b: (i, b, 0, 0, 0))
    cblk = pl.BlockSpec((None, None, n_head, HEAD_DIM, W), lambda b: (l, b, 0, 0, 0))
    full = lambda a: pl.BlockSpec(a.shape, lambda b: (0,) * a.ndim)
    vec = pl.BlockSpec((1, HEAD_DIM), lambda b: (0, 0))
    out_blk = pl.BlockSpec((1, n_head, n_row, HEAD_DIM), lambda b: (b, 0, 0, 0))
    return pl.pallas_call(
        functools.partial(_attn_s_kernel, n_near=n_near),
        grid=(B,),
        in_specs=[qblk(0), qblk(1), qblk(2), cblk, cblk,
                  full(bias_near), full(bias_far), full(bias_new), vec, vec, vec],
        out_specs=[out_blk, out_blk],
        out_shape=[jax.ShapeDtypeStruct((B, n_head, n_row, HEAD_DIM), F32),
                   jax.ShapeDtypeStruct((B, n_head, n_row, HEAD_DIM), F32)],
        scratch_shapes=[pltpu.VMEM((n_head, BAND, HEAD_DIM), F32)] * 2,
        compiler_params=_cparams("parallel"),
        name="attn_sample",
    )(qkv, qkv, qkv, cache_kt, cache_vt, bias_near, bias_far, bias_new, qg, kg, og)


def _bucket_table(n):
    dist = np.arange(n)
    exact = N_BUCKETS // 2
    df = np.maximum(dist, 1).astype(np.float32)
    large = exact + (np.log(df / np.float32(exact)) / np.float32(math.log(MAX_DISTANCE / exact))
                     * np.float32(N_BUCKETS - exact)).astype(np.int32)
    return np.where(dist < exact, dist, np.minimum(large, N_BUCKETS - 1))


def _prompt_bias_tiles(rel_bias):
    span = 3 * BAND
    tiles = []
    for _, dil in DIL_GROUPS:
        steps = np.arange(BAND, -1, -1)
        vec = rel_bias[_bucket_table(BAND * dil + 1)[steps * dil]].T.astype(F32)
        u = jnp.concatenate([vec, jnp.full((H_A, span - BAND - 1), NEG, F32)], axis=1)
        t = jnp.tile(u, (1, BAND))[:, :BAND * (span - 1)].reshape(H_A, BAND, span - 1)
        tiles.append(t[:, :, :2 * BAND])
    return jnp.stack(tiles)


def _sample_bias_tiles(rel_bias, steps, rows, wbuf):
    n_dist = wbuf + rows
    bucket = _bucket_table(n_dist)
    dist = np.arange(n_dist)
    tab = rel_bias[bucket].T.astype(F32)
    n_near = DIL_GROUPS[-2][0]
    tiles = []
    for win, dil in DIL_GROUPS:
        ok = (dist % dil == 0) & (dist >= dil) & (dist <= win)
        rev = jnp.where(ok[None, ::-1], tab[:, ::-1], NEG)
        tiles.append(jnp.stack([rev[:, rows - 1 - t:rows - 1 - t + wbuf] for t in range(rows)], axis=1))
    near = jnp.stack([t[:, :, wbuf - n_near:] for t in tiles[:-1]])
    far = tiles[-1]
    t_new = np.arange(rows)[:, None]
    tau = np.arange(BAND)[None, :]
    live = (t_new < steps) & (tau < steps)
    back = np.clip(t_new - tau, 0, steps)
    new = []
    for g, (_, dil) in enumerate(DIL_GROUPS):
        ok = live & ((tau <= t_new) if g == 0 else (tau == t_new))
        new.append(jnp.where(ok[None], rel_bias[bucket[back * dil]].transpose(2, 0, 1).astype(F32), NEG))
    return near, far, jnp.stack(new)


def kernel(x_prompt, x_sample, cache_k_win, cache_v_win, state_gla, state_conv, c_prompt, c_sample,
           rel_bias, norm1_g, norm2_g, w_ada, b_ada, w_in, q_norm_g, k_norm_g, w_gate2, b_gate2,
           a_out_g, gla_out_g, w_out, w_up, conv_w, conv_b, w_down):
    B, S, _ = x_prompt.shape
    Bs, T, _ = x_sample.shape
    wbuf = cache_k_win.shape[2]
    Mp, Ms = B * S, Bs * T
    assert wbuf == DIL_GROUPS[-1][0] and T <= SUBLANES

    w_in_t = jnp.transpose(w_in, (0, 2, 1))
    w_main = _main_cols_bf16(w_in_t)
    w_glr = jnp.pad(w_in_t[:, MAIN_COLS:], ((0, 0), (0, LANES - GATE_RANK), (0, 0))).astype(BF16)
    w_g2 = jnp.pad(w_gate2, ((0, 0), (0, LANES - GATE_RANK), (0, 0))).astype(BF16)
    b_g2 = b_gate2.reshape(DEPTH, 1, KEY_B)
    w_out_b = w_out.astype(BF16)
    w_up_b = w_up.astype(BF16)
    w_down_b = w_down.astype(BF16)
    col_scale = jnp.ones((1, MAIN_COLS), F32).at[:, 3 * MIX_A:3 * MIX_A + KEY_B].set(DK_B ** -0.5)
    vec3 = lambda a: a.reshape(DEPTH, 1, -1)
    pair = lambda a, l: jnp.tile(a[l], LANES // HEAD_DIM).reshape(1, LANES)
    head = lambda a, l: a[l].reshape(1, HEAD_DIM)

    mods = _ada_mods(jnp.concatenate([c_prompt, c_sample], axis=0), w_ada, b_ada)

    bias_p = _prompt_bias_tiles(rel_bias)
    bias_near, bias_far, bias_new = _sample_bias_tiles(rel_bias, T, SUBLANES, wbuf)

    cache_kt = jnp.transpose(cache_k_win, (0, 1, 3, 4, 2))
    cache_vt = jnp.transpose(cache_v_win, (0, 1, 3, 4, 2))
    hist_p = jnp.zeros((B, CONV_W - 1, 1, D_FF), F32)

    xp = x_prompt.reshape(Mp, D_MODEL)
    xs = jnp.transpose(x_sample, (1, 0, 2)).reshape(Ms, D_MODEL)
    outs = [[] for _ in range(6)]
    kv_p = ()
    for l in range(DEPTH):
        mod_p = [m.reshape(B, 1, D_MODEL) for m in jnp.split(mods[l, :B], 6, axis=-1)]
        mod_s = [jnp.tile(m, (T, 1)).reshape(1, Ms, D_MODEL) for m in jnp.split(mods[l, B:], 6, axis=-1)]

        proj, log_a = _in_proj(l, xp, vec3(norm1_g), mod_p[1], mod_p[0], w_main, col_scale, w_glr, w_g2, b_g2,
                               tm=TM_IN, rows_per_group=S)
        proj = proj.reshape(B, S, MAIN_COLS)
        *kv_p, oa = _attn_prompt(l, proj, bias_p, pair(q_norm_g, l), pair(k_norm_g, l), pair(a_out_g, l), kv_p)
        ob, g_st = _gla_prompt(proj, log_a.reshape(B, S, KEY_B), gla_out_g[l].reshape(1, DV_B))
        xp, h2 = _out_proj(l, oa.reshape(Mp, MIX_A), ob.reshape(Mp, MIX_B), xp, mod_p[2], w_out_b,
                           vec3(norm2_g), mod_p[4], mod_p[3], tm=TM_OUT, rows_per_group=S)
        act, tail = _up_proj(l, h2, w_up_b, conv_w, vec3(conv_b), hist_p, n_seq=1, steps=S, tn=TN_UP)
        xp = _down_proj(l, act, w_down_b, xp, mod_p[5], tm=TM_DOWN, tn=TN_DOWN, rows_per_group=S)
        outs[0].append(g_st)
        outs[1].append(tail[:, tail.shape[1] - (CONV_W - 1):])

        proj, log_a = _in_proj(l, xs, vec3(norm1_g), mod_s[1], mod_s[0], w_main, col_scale, w_glr, w_g2, b_g2,
                               tm=Ms, rows_per_group=Ms)
        bm = lambda a: jnp.transpose(a.reshape(T, Bs, -1), (1, 0, 2))
        qkv = jnp.transpose(proj[:, :3 * MIX_A].reshape(T, Bs, 3, H_A, HEAD_DIM), (2, 1, 3, 0, 4))
        qkv_pad = jnp.pad(qkv, ((0, 0), (0, 0), (0, 0), (0, SUBLANES - T), (0, 0)))
        kn_s, oa_s = _attn_sample(l, qkv_pad, cache_kt, cache_vt, bias_near, bias_far, bias_new,
                                  head(q_norm_g, l), head(k_norm_g, l), head(a_out_g, l))
        gla_in = bm(proj[:, 3 * MIX_A:])
        ob_s, g_st = _gla_sample(l, gla_in[..., :KEY_B], gla_in[..., KEY_B:2 * KEY_B],
                                 gla_in[..., 2 * KEY_B:2 * KEY_B + MIX_B], bm(log_a),
                                 gla_in[..., 2 * KEY_B + MIX_B:], gla_out_g[l].reshape(1, DV_B), state_gla)
        oa_t = jnp.transpose(oa_s[:, :, :T], (2, 0, 1, 3)).reshape(Ms, MIX_A).astype(BF16)
        ob_t = jnp.transpose(ob_s, (1, 0, 2)).reshape(Ms, MIX_B).astype(BF16)
        xs, h2 = _out_proj(l, oa_t, ob_t, xs, mod_s[2], w_out_b,
                           vec3(norm2_g), mod_s[4], mod_s[3], tm=Ms, rows_per_group=Ms)
        hist_s = jnp.transpose(state_conv[l], (1, 0, 2)).reshape(1, CONV_W - 1, Bs, D_FF)
        act, tail = _up_proj(l, h2, w_up_b, conv_w, vec3(conv_b), hist_s, n_seq=Bs, steps=T, tn=TN_UP)
        xs = _down_proj(l, act, w_down_b, xs, mod_s[5], tm=Ms, tn=TN_DOWN, rows_per_group=Ms)
        outs[2].append(jnp.transpose(kn_s[:, :, :T], (0, 2, 1, 3)))
        outs[3].append(jnp.transpose(qkv[2], (0, 2, 1, 3)))
        outs[4].append(g_st)
        outs[5].append(jnp.transpose(tail.reshape(CONV_W - 1, Bs, D_FF), (1, 0, 2)))

    y_prompt = xp.reshape(B, S, D_MODEL)
    y_sample = jnp.transpose(xs.reshape(T, Bs, D_MODEL), (1, 0, 2))
    kv_prompt = tuple(jnp.transpose(a, (0, 1, 4, 2, 3)) for a in kv_p)
    return (y_prompt, y_sample) + kv_prompt + tuple(jnp.stack(o) for o in outs)
```

```python
import functools
import math

import numpy as np
import jax
import jax.numpy as jnp
from jax import lax
from jax.experimental import pallas as pl
from jax.experimental.pallas import tpu as pltpu

F32 = jnp.float32
BF16 = jnp.bfloat16

D_MODEL = 2048
DEPTH = 2
HEAD_DIM = 64
MIX_A = D_MODEL // 2
H_A = MIX_A // HEAD_DIM
DIL_GROUPS = ((128, 1), (512, 4), (2048, 16))
N_BUCKETS = 32
MAX_DISTANCE = 2048
MIX_B = D_MODEL - MIX_A
H_B = 4
DV_B = MIX_B // H_B
DK_B = DV_B // 2
KEY_B = H_B * DK_B
GATE_RANK = 16
GATE_TAU = 16.0
D_FF = ((8 * D_MODEL // 3 + 255) // 256) * 256
CONV_W = 3
EPS = 1e-6
ATTN_SCALE = HEAD_DIM ** -0.5
MAIN_COLS = 3 * MIX_A + 2 * KEY_B + 2 * MIX_B
BAND = 128
NEG = -1e30

LANES = 128
SUBLANES = 8
VMEM_LIMIT = 56 * 1024 * 1024
TM_IN, TM_OUT, TM_DOWN = 1024, 512, 1024
TN_IN, TN_UP, TN_DOWN = 1536, 512, 512
UP_SPLIT, OUT_SPLIT = 2, 2
GLA_BLOCK = 128
GLA_SUB = 16
ATTN_PAR = 8
GLA_UNROLL = 4
GLA_PAR = 2

_NT = (((1,), (1,)), ((), ()))


def _cparams(*sem):
    return pltpu.CompilerParams(dimension_semantics=sem, vmem_limit_bytes=VMEM_LIMIT)


def _dot(a, b):
    return jnp.dot(a, b, preferred_element_type=F32)


def _dot_nt(a, b):
    return lax.dot_general(a, b, _NT, preferred_element_type=F32)


def _silu(x):
    return x * jax.nn.sigmoid(x)


def _rms(x, g):
    return x * lax.rsqrt(jnp.mean(x * x, axis=-1, keepdims=True) + EPS) * g


def _ada_kernel(c_ref, w_ref, b_ref, o_ref):
    a = _silu(c_ref[...]).astype(BF16)
    o_ref[...] = _dot(a, w_ref[...].astype(BF16)) + b_ref[...]


def _ada_mods(c_all, w_ada, b_ada):
    G = c_all.shape[0]
    tn = 1024
    return pl.pallas_call(
        _ada_kernel,
        grid=(DEPTH, 6 * D_MODEL // tn),
        in_specs=[pl.BlockSpec((G, D_MODEL), lambda l, j: (0, 0)),
                  pl.BlockSpec((None, D_MODEL, tn), lambda l, j: (l, 0, j)),
                  pl.BlockSpec((None, 1, tn), lambda l, j: (l, 0, j))],
        out_specs=pl.BlockSpec((None, G, tn), lambda l, j: (l, 0, j)),
        out_shape=jax.ShapeDtypeStruct((DEPTH, G, 6 * D_MODEL), F32),
        compiler_params=_cparams("parallel", "parallel"),
        name="ada_mods",
    )(c_all, w_ada, b_ada.reshape(DEPTH, 1, 6 * D_MODEL))


def _cast_kernel(x_ref, o_ref):
    o_ref[...] = x_ref[...].astype(BF16)


def _main_cols_bf16(w_in_t):
    tn = 1024
    return pl.pallas_call(
        _cast_kernel,
        grid=(DEPTH, MAIN_COLS // tn),
        in_specs=[pl.BlockSpec((None, tn, D_MODEL), lambda l, j: (l, j, 0))],
        out_specs=pl.BlockSpec((None, tn, D_MODEL), lambda l, j: (l, j, 0)),
        out_shape=jax.ShapeDtypeStruct((DEPTH, MAIN_COLS, D_MODEL), BF16),
        compiler_params=_cparams("parallel", "parallel"),
        name="w_in_bf16",
    )(w_in_t)


def _in_kernel(x_ref, g_ref, sc_ref, sh_ref, w_ref, cs_ref, wg_ref, wg2_ref, bg2_ref,
               o_ref, la_ref, h_scr):
    @pl.when(pl.program_id(1) == 0)
    def _():
        hb = (_rms(x_ref[...], g_ref[...]) * (1.0 + sc_ref[...]) + sh_ref[...]).astype(BF16)
        h_scr[...] = hb
        glr = _dot_nt(hb, wg_ref[...])
        z = _dot(glr.astype(BF16), wg2_ref[...]) + bg2_ref[...]
        la_ref[...] = (jnp.minimum(z, 0.0) - jnp.log(1.0 + jnp.exp(-jnp.abs(z)))) * (1.0 / GATE_TAU)

    o_ref[...] = _dot_nt(h_scr[...], w_ref[...]) * cs_ref[...]


def _in_proj(l, x, norm_g, scale, shift, w_main, col_scale, w_glr, w_g2, b_g2, tm, rows_per_group):
    M = x.shape[0]
    tn = TN_IN
    R = scale.shape[1]
    grp = lambda i, j: ((i * tm) // rows_per_group, 0, 0)
    return pl.pallas_call(
        _in_kernel,
        grid=(M // tm, MAIN_COLS // tn),
        in_specs=[pl.BlockSpec((tm, D_MODEL), lambda i, j: (i, 0)),
                  pl.BlockSpec((None, 1, D_MODEL), lambda i, j: (l, 0, 0)),
                  pl.BlockSpec((None, R, D_MODEL), grp),
                  pl.BlockSpec((None, R, D_MODEL), grp),
                  pl.BlockSpec((None, tn, D_MODEL), lambda i, j: (l, j, 0)),
                  pl.BlockSpec((1, tn), lambda i, j: (0, j)),
                  pl.BlockSpec((None, LANES, D_MODEL), lambda i, j: (l, 0, 0)),
                  pl.BlockSpec((None, LANES, KEY_B), lambda i, j: (l, 0, 0)),
                  pl.BlockSpec((None, 1, KEY_B), lambda i, j: (l, 0, 0))],
        out_specs=[pl.BlockSpec((tm, tn), lambda i, j: (i, j)),
                   pl.BlockSpec((tm, KEY_B), lambda i, j: (i, 0))],
        out_shape=[jax.ShapeDtypeStruct((M, MAIN_COLS), F32),
                   jax.ShapeDtypeStruct((M, KEY_B), F32)],
        scratch_shapes=[pltpu.VMEM((tm, D_MODEL), BF16)],
        compiler_params=_cparams("parallel", "arbitrary"),
        name="in_proj",
    )(x, norm_g, scale, shift, w_main, col_scale, w_glr, w_g2, b_g2)


def _out_kernel(oa_ref, ob_ref, x_ref, g1_ref, wa_ref, wb_ref, n2_ref, sc_ref, sh_ref, xo_ref, h2_ref):
    tm = x_ref.shape[0]
    halves = [pl.ds(c * (tm // OUT_SPLIT), tm // OUT_SPLIT) for c in range(OUT_SPLIT)]
    ys = [_dot(oa_ref[rs, :], wa_ref[...]) + _dot(ob_ref[rs, :], wb_ref[...]) for rs in halves]
    per_row = g1_ref.shape[0] == tm
    for rs, y in zip(halves, ys):
        mod = lambda ref: ref[rs, :] if per_row else ref[...]
        xn = x_ref[rs, :] + mod(g1_ref) * y
        xo_ref[rs, :] = xn
        h2_ref[rs, :] = (_rms(xn, n2_ref[...]) * (1.0 + mod(sc_ref)) + mod(sh_ref)).astype(BF16)


def _out_proj(l, oa, ob, x, gate1, w_out, norm2_g, scale2, shift2, tm, rows_per_group):
    M = x.shape[0]
    R = gate1.shape[1]
    grp = lambda i: ((i * tm) // rows_per_group, 0, 0)
    return pl.pallas_call(
        _out_kernel,
        grid=(M // tm,),
        in_specs=[pl.BlockSpec((tm, MIX_A), lambda i: (i, 0)),
                  pl.BlockSpec((tm, MIX_B), lambda i: (i, 0)),
                  pl.BlockSpec((tm, D_MODEL), lambda i: (i, 0)),
                  pl.BlockSpec((None, R, D_MODEL), grp),
                  pl.BlockSpec((None, MIX_A, D_MODEL), lambda i: (0, 0, 0)),
                  pl.BlockSpec((None, MIX_B, D_MODEL), lambda i: (0, 1, 0)),
                  pl.BlockSpec((None, 1, D_MODEL), lambda i: (l, 0, 0)),
                  pl.BlockSpec((None, R, D_MODEL), grp),
                  pl.BlockSpec((None, R, D_MODEL), grp)],
        out_specs=[pl.BlockSpec((tm, D_MODEL), lambda i: (i, 0)),
                   pl.BlockSpec((tm, D_MODEL), lambda i: (i, 0))],
        out_shape=[jax.ShapeDtypeStruct((M, D_MODEL), F32),
                   jax.ShapeDtypeStruct((M, D_MODEL), BF16)],
        compiler_params=_cparams("parallel"),
        name="out_proj",
    )(oa, ob, x, gate1, w_out, w_out, norm2_g, scale2, shift2)


def _up_kernel(h_ref, wg_ref, wu_ref, cw_ref, cb_ref, hist_ref, act_ref, tail_ref, *, n_seq, steps):
    h = h_ref[...]
    rows = n_seq * steps
    tail = tail_ref.shape[0]
    tn = act_ref.shape[1]
    halves = [slice(c * (tn // UP_SPLIT), (c + 1) * (tn // UP_SPLIT)) for c in range(UP_SPLIT)]
    gs = [_dot(h, wg_ref[:, cs]) for cs in halves]
    gates = []
    for cs, g in zip(halves, gs):
        row = lax.broadcasted_iota(jnp.int32, g.shape, 0)

        def per_row(hist, g=g):
            if n_seq == 1:
                return jnp.broadcast_to(hist, g.shape)
            return jnp.concatenate([hist] * steps, axis=0)

        h0 = per_row(hist_ref[0, :, cs])
        h1 = per_row(hist_ref[1, :, cs])
        g1 = jnp.where(row < n_seq, h1, pltpu.roll(g, n_seq, 0))
        g2 = jnp.where(row < n_seq, h0, jnp.where(row < 2 * n_seq, h1, pltpu.roll(g, 2 * n_seq, 0)))
        cw = cw_ref[:, cs]
        conv = cb_ref[:, cs] + cw[0:1] * g2 + cw[1:2] * g1 + cw[2:3] * g
        gates.append(_silu(conv))
        tail_ref[:, cs] = g[rows - tail:]
    for cs, gate in zip(halves, gates):
        act_ref[:, cs] = (gate * _dot(h, wu_ref[:, cs])).astype(BF16)


def _up_proj(l, h2, w_up, conv_w, conv_b, hist, n_seq, steps, tn):
    M = h2.shape[0]
    tm = n_seq * steps
    nj = D_FF // tn
    tail = max(SUBLANES, (CONV_W - 1) * n_seq)
    return pl.pallas_call(
        functools.partial(_up_kernel, n_seq=n_seq, steps=steps),
        grid=(M // tm, nj),
        in_specs=[pl.BlockSpec((tm, D_MODEL), lambda i, j: (i, 0)),
                  pl.BlockSpec((None, D_MODEL, tn), lambda i, j: (0, 0, j)),
                  pl.BlockSpec((None, D_MODEL, tn), lambda i, j: (0, 0, nj + j)),
                  pl.BlockSpec((None, CONV_W, tn), lambda i, j: (l, 0, j)),
                  pl.BlockSpec((None, 1, tn), lambda i, j: (l, 0, j)),
                  pl.BlockSpec((None, CONV_W - 1, n_seq, tn), lambda i, j: (i, 0, 0, j))],
        out_specs=[pl.BlockSpec((tm, tn), lambda i, j: (i, j)),
                   pl.BlockSpec((None, tail, tn), lambda i, j: (i, 0, j))],
        out_shape=[jax.ShapeDtypeStruct((M, D_FF), BF16),
                   jax.ShapeDtypeStruct((M // tm, tail, D_FF), F32)],
        compiler_params=_cparams("parallel", "parallel"),
        name="up_proj",
    )(h2, w_up, w_up, conv_w, conv_b, hist)


def _down_kernel(a_ref, w_ref, x_ref, g2_ref, o_ref):
    o_ref[...] = x_ref[...] + g2_ref[...] * _dot(a_ref[...], w_ref[...])


def _down_proj(act, w_down, x, gate2, tm, tn, rows_per_group):
    M = x.shape[0]
    R = gate2.shape[1]
    return pl.pallas_call(
        _down_kernel,
        grid=(M // tm, D_MODEL // tn),
        in_specs=[pl.BlockSpec((tm, D_FF), lambda i, j: (i, 0)),
                  pl.BlockSpec((None, D_FF, tn), lambda i, j: (0, 0, j)),
                  pl.BlockSpec((tm, tn), lambda i, j: (i, j)),
                  pl.BlockSpec((None, R, tn), lambda i, j: ((i * tm) // rows_per_group, 0, j))],
        out_specs=pl.BlockSpec((tm, tn), lambda i, j: (i, j)),
        out_shape=jax.ShapeDtypeStruct((M, D_MODEL), F32),
        compiler_params=_cparams("parallel", "parallel"),
        name="down_proj",
    )(act, w_down, x, gate2)


def _head_pair_norm(x, g, same_head):
    x2 = x * x
    hi = x2.astype(BF16)
    low = (x2 - hi.astype(F32)).astype(BF16)
    ms = (_dot(hi, same_head) + _dot(low, same_head)) * (1.0 / HEAD_DIM)
    return x * lax.rsqrt(ms + EPS) * g


def _attn_p_kernel(q_ref, k_ref, v_ref, bias_ref, qg_ref, kg_ref, og_ref, wo_ref, wu_ref, wd_ref, *rest,
                   seq, n_alias):
    (knt_ref, vt_ref, oa_ref, wo_out, wu_out, wd_out,
     qs_scr, ks_scr, perm_scr, m_scr, l_scr, acc_scr) = rest[n_alias:]
    for src, dst in ((wo_ref, wo_out), (wu_ref, wu_out), (wd_ref, wd_out)):
        dst[...] = src[...].astype(BF16)
    lo = lax.broadcasted_iota(jnp.int32, (1, LANES), 1) < HEAD_DIM
    head_of = lambda axis: lax.broadcasted_iota(jnp.int32, (LANES, LANES), axis) // HEAD_DIM
    same_head = jnp.where(head_of(0) == head_of(1), 1.0, 0.0).astype(BF16)
    kn = _head_pair_norm(k_ref[0], kg_ref[...], same_head)
    ks_scr[...] = kn
    kn_t = kn.T
    for ref, val_t in ((knt_ref, kn_t), (vt_ref, v_ref[0].T)):
        if n_alias:
            ref[0] = val_t.reshape(2, HEAD_DIM, seq)
        else:
            ref[0, 0] = val_t.reshape(2, HEAD_DIM, seq)
            ref[1:, 0] = jnp.zeros((ref.shape[0] - 1, 2, HEAD_DIM, seq), F32)
    qs_scr[...] = _head_pair_norm(q_ref[0], qg_ref[...], same_head) * ATTN_SCALE
    col = lax.broadcasted_iota(jnp.int32, (BAND, 2 * BAND), 1)

    stage_rows = [(lambda rs: qs_scr[rs, :], lambda rs: ks_scr[rs, :], lambda rs: v_ref[0, rs, :])]
    for g in range(1, len(DIL_GROUPS)):
        d_prev, d_cur = DIL_GROUPS[g - 1][1], DIL_GROUPS[g][1]
        ratio = d_cur // d_prev
        len_prev, len_cur = seq // d_prev, seq // d_cur
        for t in range(3):
            for c in range(d_prev):
                for m in range(ratio):
                    src = pl.ds(c * len_prev + m, len_cur, stride=ratio)
                    perm_scr[g - 1, t, pl.ds((c + d_prev * m) * len_cur, len_cur), :] = stage_rows[g - 1][t](src)
        stage_rows.append(tuple((lambda rs, g=g, t=t: perm_scr[g - 1, t, rs, :]) for t in range(3)))

    for g, (_, dil) in enumerate(DIL_GROUPS):
        n_blk = seq // dil // BAND
        has_prev = n_blk > 1
        q_rows, k_rows, v_rows = stage_rows[g]

        def rows(start, dil=dil):
            return pl.ds(start, BAND, stride=dil) if dil > 1 else pl.ds(start, BAND)

        def body(trip, carry, g=g, dil=dil, n_blk=n_blk, rows=rows, has_prev=has_prev,
                 q_rows=q_rows, k_rows=k_rows, v_rows=v_rows):
            blocks = []
            for u in range(ATTN_PAR):
                idx = trip * ATTN_PAR + u
                r = idx // n_blk
                nb = idx % n_blk
                cur = pl.ds(pl.multiple_of((r * n_blk + nb) * BAND, BAND), BAND)
                qb = q_rows(cur)
                kcat = k_rows(cur)
                vcat = v_rows(cur)
                bias = [bias_ref[g, h, :, BAND:] for h in range(2)]
                if has_prev:
                    prev = pl.ds(pl.multiple_of((r * n_blk + jnp.maximum(nb - 1, 0)) * BAND, BAND), BAND)
                    kcat = jnp.concatenate([k_rows(prev), kcat], axis=0)
                    vcat = jnp.concatenate([v_rows(prev), vcat], axis=0)
                    no_prev = jnp.logical_and(nb == 0, col < BAND)
                    bias = [jnp.where(no_prev, NEG, bias_ref[g, h]) for h in range(2)]
                blocks.append(dict(out=rows(r + nb * (BAND * dil)), q=qb, k=kcat.astype(BF16),
                                   v=vcat.astype(BF16), bias=bias))
            heads = [(blk, h) for blk in blocks for h in range(2)]
            s = [_dot_nt(jnp.where(lo if h == 0 else jnp.logical_not(lo), blk["q"], 0.0).astype(BF16), blk["k"])
                 + blk["bias"][h] for blk, h in heads]
            mx = [jnp.max(x, axis=-1, keepdims=True) for x in s]
            p = [jnp.exp(x - m) for x, m in zip(s, mx)]
            sm = [jnp.sum(x, axis=-1, keepdims=True) for x in p]
            o = [_dot(x.astype(BF16), blk["v"]) for x, (blk, _) in zip(p, heads)]
            for u, blk in enumerate(blocks):
                m_scr[g, blk["out"], :] = jnp.where(lo, mx[2 * u], mx[2 * u + 1])
                l_scr[g, blk["out"], :] = jnp.where(lo, sm[2 * u], sm[2 * u + 1])
                acc_scr[g, blk["out"], :] = jnp.where(lo, o[2 * u], o[2 * u + 1])
            return carry

        lax.fori_loop(0, seq // BAND // ATTN_PAR, body, 0)

    n_g = len(DIL_GROUPS)
    m_all = m_scr[0]
    for g in range(1, n_g):
        m_all = jnp.maximum(m_all, m_scr[g])
    num = jnp.zeros((seq, LANES), F32)
    den = jnp.zeros((seq, LANES), F32)
    for g in range(n_g):
        w = jnp.exp(m_scr[g] - m_all)
        num = num + w * acc_scr[g]
        den = den + w * l_scr[g]
    oa_ref[0] = _head_pair_norm(num / den, og_ref[...], same_head).astype(BF16)


def _attn_prompt(l, proj, bias_tiles, qg, kg, og, weights, stacked):
    B, S, _ = proj.shape
    assert bool(stacked) == (l > 0)
    n_pair = MIX_A // LANES
    n_g = len(DIL_GROUPS)
    blk = lambda off: pl.BlockSpec((1, S, LANES), lambda b, hp: (b, 0, off + hp))
    vec = pl.BlockSpec((1, LANES), lambda b, hp: (0, 0))
    if stacked:
        t_blk = pl.BlockSpec((None, 1, 2, HEAD_DIM, S), lambda b, hp: (l, b, hp, 0, 0))
    else:
        t_blk = pl.BlockSpec((DEPTH, 1, 2, HEAD_DIM, S), lambda b, hp: (0, b, hp, 0, 0))
    w_tile = lambda w: (w.shape[1] // B, w.shape[2] // n_pair)
    w_in_specs = [pl.BlockSpec((None,) + w_tile(w), lambda b, hp: (l, b, hp)) for w in weights]
    w_out_specs = [pl.BlockSpec((None,) + w_tile(w), lambda b, hp: (0, b, hp)) for w in weights]
    n_in = 7 + len(weights)
    return pl.pallas_call(
        functools.partial(_attn_p_kernel, seq=S, n_alias=len(stacked)),
        grid=(B, n_pair),
        in_specs=[blk(0), blk(n_pair), blk(2 * n_pair),
                  pl.BlockSpec((n_g, 2, BAND, 2 * BAND), lambda b, hp: (0, hp, 0, 0)),
                  vec, vec, vec] + w_in_specs + [pl.BlockSpec(memory_space=pl.ANY)] * len(stacked),
        out_specs=[t_blk, t_blk, pl.BlockSpec((1, S, LANES), lambda b, hp: (b, 0, hp))] + w_out_specs,
        out_shape=[jax.ShapeDtypeStruct((DEPTH, B, H_A, HEAD_DIM, S), F32),
                   jax.ShapeDtypeStruct((DEPTH, B, H_A, HEAD_DIM, S), F32),
                   jax.ShapeDtypeStruct((B, S, MIX_A), BF16)]
                  + [jax.ShapeDtypeStruct((1,) + w.shape[1:], BF16) for w in weights],
        input_output_aliases={n_in + i: i for i in range(len(stacked))},
        scratch_shapes=([pltpu.VMEM((S, LANES), F32)] * 2 + [pltpu.VMEM((n_g - 1, 3, S, LANES), F32)]
                        + [pltpu.VMEM((n_g, S, LANES), F32)] * 3),
        compiler_params=_cparams("parallel", "parallel"),
        name="attn_prompt",
    )(proj, proj, proj, bias_tiles, qg, kg, og, *weights, *stacked)


def _gla_p_kernel(q_ref, k_ref, v_ref, la_ref, r_ref, g_ref, ob_ref, st_ref, st_scr, *, seq):
    C = GLA_BLOCK
    n_sub = C // GLA_SUB
    st_scr[...] = jnp.zeros(st_scr.shape, F32)
    row = lax.broadcasted_iota(jnp.int32, (C, C), 0)
    tri = row >= lax.broadcasted_iota(jnp.int32, (C, C), 1)
    tri_b = jnp.where(tri, 1.0, 0.0).astype(BF16)
    krow = lax.broadcasted_iota(jnp.int32, (C, DK_B), 0)

    heads = range(GLA_PAR)
    ksl = [slice(DK_B * h, DK_B * (h + 1)) for h in heads]
    vsl = [slice(DV_B * h, DV_B * (h + 1)) for h in heads]

    def block(c, carry):
        rs = pl.ds(pl.multiple_of(c * C, C), C)
        la = [la_ref[0, rs, ksl[h]] for h in heads]
        hi = [x.astype(BF16) for x in la]
        r1 = [x - y.astype(F32) for x, y in zip(la, hi)]
        mid = [x.astype(BF16) for x in r1]
        low = [(x - y.astype(F32)).astype(BF16) for x, y in zip(r1, mid)]
        cum = [_dot(tri_b, a) + _dot(tri_b, b) + _dot(tri_b, d) for a, b, d in zip(hi, mid, low)]
        q = [q_ref[0, rs, ksl[h]] for h in heads]
        k = [k_ref[0, rs, ksl[h]] for h in heads]
        v = [v_ref[0, rs, vsl[h]] for h in heads]
        starts = [[(cum[h] - la[h])[GLA_SUB * i:GLA_SUB * i + 1, :] for i in range(n_sub)] for h in heads]
        start_rows = [jnp.concatenate([jnp.broadcast_to(s, (GLA_SUB, DK_B)) for s in starts[h]], axis=0)
                      for h in heads]
        qe = [(q[h] * jnp.exp(cum[h] - start_rows[h])).astype(BF16) for h in heads]
        khat = [[(k[h] * jnp.exp(jnp.where(krow < GLA_SUB * (i + 1), starts[h][i] - cum[h], NEG))).astype(BF16)
                 for i in range(n_sub)] for h in heads]
        att = [jnp.concatenate([_dot_nt(qe[h][GLA_SUB * i:GLA_SUB * (i + 1)], khat[h][i]) for i in range(n_sub)],
                               axis=0) for h in heads]
        att = [jnp.where(tri, x, 0.0).astype(BF16) for x in att]
        o = [_dot(att[h], v[h].astype(BF16))
             + _dot_nt((q[h] * jnp.exp(cum[h])).astype(BF16), st_scr[h].astype(BF16)) for h in heads]
        last = [x[C - 1:C, :] for x in cum]
        kd = [(k[h] * jnp.exp(last[h] - cum[h])).astype(BF16) for h in heads]
        upd = [_dot(v[h].T.astype(BF16), kd[h]) for h in heads]
        for h in heads:
            st_scr[h] = st_scr[h] * jnp.exp(last[h]) + upd[h]
            ob_ref[0, rs, vsl[h]] = (_rms(o[h], g_ref[...]) * _silu(r_ref[0, rs, vsl[h]])).astype(BF16)
        return carry

    lax.fori_loop(0, seq // C, block, 0, unroll=GLA_UNROLL)
    for h in heads:
        st_ref[0, h] = st_scr[h].T


def _gla_prompt(proj, log_a, gla_g):
    B, S, _ = proj.shape
    kw, vw = GLA_PAR * DK_B, GLA_PAR * DV_B
    q0 = 3 * MIX_A // kw
    k0 = q0 + KEY_B // kw
    v0 = (3 * MIX_A + 2 * KEY_B) // vw
    r0 = v0 + MIX_B // vw
    kblk = lambda off: pl.BlockSpec((1, S, kw), lambda b, h: (b, 0, off + h))
    vblk = lambda off: pl.BlockSpec((1, S, vw), lambda b, h: (b, 0, off + h))
    return pl.pallas_call(
        functools.partial(_gla_p_kernel, seq=S),
        grid=(B, H_B // GLA_PAR),
        in_specs=[kblk(q0), kblk(k0), vblk(v0), kblk(0), vblk(r0),
                  pl.BlockSpec((1, DV_B), lambda b, h: (0, 0))],
        out_specs=[vblk(0), pl.BlockSpec((1, GLA_PAR, DK_B, DV_B), lambda b, h: (b, h, 0, 0))],
        out_shape=[jax.ShapeDtypeStruct((B, S, MIX_B), BF16),
                   jax.ShapeDtypeStruct((B, H_B, DK_B, DV_B), F32)],
        scratch_shapes=[pltpu.VMEM((GLA_PAR, DV_B, DK_B), F32)],
        compiler_params=_cparams("parallel", "parallel"),
        name="gla_prompt",
    )(proj, proj, proj, log_a, proj, gla_g)


def _gla_s_kernel(q_ref, k_ref, v_ref, la_ref, r_ref, g_ref, s_ref, ob_ref, so_ref, *, steps):
    rowi = lax.broadcasted_iota(jnp.int32, (DK_B, DK_B), 0)
    heads = range(H_B)
    ks = [slice(DK_B * h, DK_B * (h + 1)) for h in heads]
    vs = [slice(DV_B * h, DV_B * (h + 1)) for h in heads]
    x = [jnp.zeros((DK_B, DK_B), F32) for _ in heads]
    for t in range(steps):
        x = [jnp.where(rowi == t, k_ref[0, t:t + 1, ks[h]], x[h]) for h in heads]
        x = [jnp.where(rowi == steps + t, jnp.exp(la_ref[0, t:t + 1, ks[h]]), x[h]) for h in heads]
        x = [jnp.where(rowi == 2 * steps + t, q_ref[0, t:t + 1, ks[h]], x[h]) for h in heads]
    xt = [a.T for a in x]
    st = [s_ref[0, h] for h in heads]
    for t in range(steps):
        st = [xt[h][:, steps + t:steps + t + 1] * st[h] + xt[h][:, t:t + 1] * v_ref[0, t:t + 1, vs[h]]
              for h in heads]
        o = [jnp.sum(xt[h][:, 2 * steps + t:2 * steps + t + 1] * st[h], axis=0, keepdims=True) for h in heads]
        for h in heads:
            ob_ref[0, t:t + 1, vs[h]] = _rms(o[h], g_ref[...]) * _silu(r_ref[0, t:t + 1, vs[h]])
    for h in heads:
        so_ref[0, h] = st[h]


def _gla_sample(l, q, k, v, log_a, r, gla_g, state):
    B, T, _ = q.shape
    kb = pl.BlockSpec((1, T, KEY_B), lambda b: (b, 0, 0))
    vb = pl.BlockSpec((1, T, MIX_B), lambda b: (b, 0, 0))
    return pl.pallas_call(
        functools.partial(_gla_s_kernel, steps=T),
        grid=(B,),
        in_specs=[kb, kb, vb, kb, vb, pl.BlockSpec((1, DV_B), lambda b: (0, 0)),
                  pl.BlockSpec((None, 1, H_B, DK_B, DV_B), lambda b: (l, b, 0, 0, 0))],
        out_specs=[vb, pl.BlockSpec((1, H_B, DK_B, DV_B), lambda b: (b, 0, 0, 0))],
        out_shape=[jax.ShapeDtypeStruct((B, T, MIX_B), F32),
                   jax.ShapeDtypeStruct((B, H_B, DK_B, DV_B), F32)],
        compiler_params=_cparams("parallel"),
        name="gla_sample",
    )(q, k, v, log_a, r, gla_g, state)


def _attn_s_kernel(q_ref, k_ref, v_ref, kt_ref, vt_ref, bnear_ref, bfar_ref, bnew_ref,
                   qg_ref, kg_ref, og_ref, kn_ref, oa_ref, kpad_scr, vpad_scr, *, n_near):
    n_head, n_row, _ = kn_ref.shape[1:]
    width = kt_ref.shape[-1]
    kn = _rms(k_ref[0], kg_ref[...])
    kn_ref[0] = kn
    qs = _rms(q_ref[0], qg_ref[...]) * ATTN_SCALE
    kpad_scr[...] = jnp.zeros(kpad_scr.shape, F32)
    vpad_scr[...] = jnp.zeros(vpad_scr.shape, F32)
    kpad_scr[:, 0:n_row, :] = kn
    vpad_scr[:, 0:n_row, :] = v_ref[0]
    n_g = len(DIL_GROUPS)
    for h in range(n_head):
        qh = qs[h].astype(BF16)
        s = _dot(qh, kt_ref[h].astype(BF16))
        s_new = _dot_nt(qh, kpad_scr[h].astype(BF16))
        far = s + bfar_ref[h]
        s_near = s[:, width - n_near:]
        near = [s_near + bnear_ref[g, h] for g in range(n_g - 1)]
        new = [s_new + bnew_ref[g, h] for g in range(n_g)]
        m = jnp.max(far, axis=-1, keepdims=True)
        for x in near + new:
            m = jnp.maximum(m, jnp.max(x, axis=-1, keepdims=True))
        p_far = jnp.exp(far - m)
        p_near = sum(jnp.exp(x - m) for x in near)
        p_new = sum(jnp.exp(x - m) for x in new)
        den = (jnp.sum(p_far, axis=-1, keepdims=True) + jnp.sum(p_near, axis=-1, keepdims=True)
               + jnp.sum(p_new, axis=-1, keepdims=True))
        p = jnp.concatenate([p_far[:, :width - n_near], p_far[:, width - n_near:] + p_near], axis=1)
        o = _dot_nt(p.astype(BF16), vt_ref[h].astype(BF16)) + _dot(p_new.astype(BF16), vpad_scr[h].astype(BF16))
        oa_ref[0, h] = _rms(o / den, og_ref[...])


def _attn_sample(l, qkv, cache_kt, cache_vt, bias_near, bias_far, bias_new, qg, kg, og):
    _, B, n_head, n_row, _ = qkv.shape
    W = cache_kt.shape[-1]
    n_near = bias_near.shape[-1]
    qblk = lambda i: pl.BlockSpec((None, 1, n_head, n_row, HEAD_DIM), lambda b: (i, b, 0, 0, 0))
    cblk = pl.BlockSpec((None, None, n_head, HEAD_DIM, W), lambda b: (l, b, 0, 0, 0))
    full = lambda a: pl.BlockSpec(a.shape, lambda b: (0,) * a.ndim)
    vec = pl.BlockSpec((1, HEAD_DIM), lambda b: (0, 0))
    out_blk = pl.BlockSpec((1, n_head, n_row, HEAD_DIM), lambda b: (b, 0, 0, 0))
    return pl.pallas_call(
        functools.partial(_attn_s_kernel, n_near=n_near),
        grid=(B,),
        in_specs=[qblk(0), qblk(1), qblk(2), cblk, cblk,
                  full(bias_near), full(bias_far), full(bias_new), vec, vec, vec],
        out_specs=[out_blk, out_blk],
        out_shape=[jax.ShapeDtypeStruct((B, n_head, n_row, HEAD_DIM), F32),
                   jax.ShapeDtypeStruct((B, n_head, n_row, HEAD_DIM), F32)],
        scratch_shapes=[pltpu.VMEM((n_head, BAND, HEAD_DIM), F32)] * 2,
        compiler_params=_cparams("parallel"),
        name="attn_sample",
    )(qkv, qkv, qkv, cache_kt, cache_vt, bias_near, bias_far, bias_new, qg, kg, og)


def _bucket_table(n):
    dist = np.arange(n)
    exact = N_BUCKETS // 2
    df = np.maximum(dist, 1).astype(np.float32)
    large = exact + (np.log(df / np.float32(exact)) / np.float32(math.log(MAX_DISTANCE / exact))
                     * np.float32(N_BUCKETS - exact)).astype(np.int32)
    return np.where(dist < exact, dist, np.minimum(large, N_BUCKETS - 1))


def _prompt_bias_tiles(rel_bias):
    span = 3 * BAND
    tiles = []
    for _, dil in DIL_GROUPS:
        steps = np.arange(BAND, -1, -1)
        vec = rel_bias[_bucket_table(BAND * dil + 1)[steps * dil]].T.astype(F32)
        u = jnp.concatenate([vec, jnp.full((H_A, span - BAND - 1), NEG, F32)], axis=1)
        t = jnp.tile(u, (1, BAND))[:, :BAND * (span - 1)].reshape(H_A, BAND, span - 1)
        tiles.append(t[:, :, :2 * BAND])
    return jnp.stack(tiles)


def _sample_bias_tiles(rel_bias, steps, rows, wbuf):
    n_dist = wbuf + rows
    bucket = _bucket_table(n_dist)
    dist = np.arange(n_dist)
    tab = rel_bias[bucket].T.astype(F32)
    n_near = DIL_GROUPS[-2][0]
    tiles = []
    for win, dil in DIL_GROUPS:
        ok = (dist % dil == 0) & (dist >= dil) & (dist <= win)
        rev = jnp.where(ok[None, ::-1], tab[:, ::-1], NEG)
        tiles.append(jnp.stack([rev[:, rows - 1 - t:rows - 1 - t + wbuf] for t in range(rows)], axis=1))
    near = jnp.stack([t[:, :, wbuf - n_near:] for t in tiles[:-1]])
    far = tiles[-1]
    t_new = np.arange(rows)[:, None]
    tau = np.arange(BAND)[None, :]
    live = (t_new < steps) & (tau < steps)
    back = np.clip(t_new - tau, 0, steps)
    new = []
    for g, (_, dil) in enumerate(DIL_GROUPS):
        ok = live & ((tau <= t_new) if g == 0 else (tau == t_new))
        new.append(jnp.where(ok[None], rel_bias[bucket[back * dil]].transpose(2, 0, 1).astype(F32), NEG))
    return near, far, jnp.stack(new)


def kernel(x_prompt, x_sample, cache_k_win, cache_v_win, state_gla, state_conv, c_prompt, c_sample,
           rel_bias, norm1_g, norm2_g, w_ada, b_ada, w_in, q_norm_g, k_norm_g, w_gate2, b_gate2,
           a_out_g, gla_out_g, w_out, w_up, conv_w, conv_b, w_down):
    B, S, _ = x_prompt.shape
    Bs, T, _ = x_sample.shape
    wbuf = cache_k_win.shape[2]
    Mp, Ms = B * S, Bs * T
    assert wbuf == DIL_GROUPS[-1][0] and T <= SUBLANES

    w_in_t = jnp.transpose(w_in, (0, 2, 1))
    w_main = _main_cols_bf16(w_in_t)
    w_glr = jnp.pad(w_in_t[:, MAIN_COLS:], ((0, 0), (0, LANES - GATE_RANK), (0, 0))).astype(BF16)
    w_g2 = jnp.pad(w_gate2, ((0, 0), (0, LANES - GATE_RANK), (0, 0))).astype(BF16)
    b_g2 = b_gate2.reshape(DEPTH, 1, KEY_B)
    col_scale = jnp.ones((1, MAIN_COLS), F32).at[:, 3 * MIX_A:3 * MIX_A + KEY_B].set(DK_B ** -0.5)
    vec3 = lambda a: a.reshape(DEPTH, 1, -1)
    pair = lambda a, l: jnp.tile(a[l], LANES // HEAD_DIM).reshape(1, LANES)
    head = lambda a, l: a[l].reshape(1, HEAD_DIM)

    mods = _ada_mods(jnp.concatenate([c_prompt, c_sample], axis=0), w_ada, b_ada)

    bias_p = _prompt_bias_tiles(rel_bias)
    bias_near, bias_far, bias_new = _sample_bias_tiles(rel_bias, T, SUBLANES, wbuf)

    cache_kt = jnp.transpose(cache_k_win, (0, 1, 3, 4, 2))
    cache_vt = jnp.transpose(cache_v_win, (0, 1, 3, 4, 2))
    hist_p = jnp.zeros((B, CONV_W - 1, 1, D_FF), F32)

    xp = x_prompt.reshape(Mp, D_MODEL)
    xs = jnp.transpose(x_sample, (1, 0, 2)).reshape(Ms, D_MODEL)
    outs = [[] for _ in range(6)]
    kv_p = ()
    for l in range(DEPTH):
        mod_p = [m.reshape(B, 1, D_MODEL) for m in jnp.split(mods[l, :B], 6, axis=-1)]
        mod_s = [jnp.tile(m, (T, 1)).reshape(1, Ms, D_MODEL) for m in jnp.split(mods[l, B:], 6, axis=-1)]

        proj, log_a = _in_proj(l, xp, vec3(norm1_g), mod_p[1], mod_p[0], w_main, col_scale, w_glr, w_g2, b_g2,
                               tm=TM_IN, rows_per_group=S)
        proj = proj.reshape(B, S, MAIN_COLS)
        knt, vt, oa, w_out_b, w_up_b, w_down_b = _attn_prompt(
            l, proj, bias_p, pair(q_norm_g, l), pair(k_norm_g, l), pair(a_out_g, l), (w_out, w_up, w_down), kv_p)
        kv_p = (knt, vt)
        ob, g_st = _gla_prompt(proj, log_a.reshape(B, S, KEY_B), gla_out_g[l].reshape(1, DV_B))
        xp, h2 = _out_proj(l, oa.reshape(Mp, MIX_A), ob.reshape(Mp, MIX_B), xp, mod_p[2], w_out_b,
                           vec3(norm2_g), mod_p[4], mod_p[3], tm=TM_OUT, rows_per_group=S)
        act, tail = _up_proj(l, h2, w_up_b, conv_w, vec3(conv_b), hist_p, n_seq=1, steps=S, tn=TN_UP)
        xp = _down_proj(act, w_down_b, xp, mod_p[5], tm=TM_DOWN, tn=TN_DOWN, rows_per_group=S)
        outs[0].append(g_st)
        outs[1].append(tail[:, tail.shape[1] - (CONV_W - 1):])

        proj, log_a = _in_proj(l, xs, vec3(norm1_g), mod_s[1], mod_s[0], w_main, col_scale, w_glr, w_g2, b_g2,
                               tm=Ms, rows_per_group=Ms)
        bm = lambda a: jnp.transpose(a.reshape(T, Bs, -1), (1, 0, 2))
        qkv = jnp.transpose(proj[:, :3 * MIX_A].reshape(T, Bs, 3, H_A, HEAD_DIM), (2, 1, 3, 0, 4))
        qkv_pad = jnp.pad(qkv, ((0, 0), (0, 0), (0, 0), (0, SUBLANES - T), (0, 0)))
        kn_s, oa_s = _attn_sample(l, qkv_pad, cache_kt, cache_vt, bias_near, bias_far, bias_new,
                                  head(q_norm_g, l), head(k_norm_g, l), head(a_out_g, l))
        gla_in = bm(proj[:, 3 * MIX_A:])
        ob_s, g_st = _gla_sample(l, gla_in[..., :KEY_B], gla_in[..., KEY_B:2 * KEY_B],
                                 gla_in[..., 2 * KEY_B:2 * KEY_B + MIX_B], bm(log_a),
                                 gla_in[..., 2 * KEY_B + MIX_B:], gla_out_g[l].reshape(1, DV_B), state_gla)
        oa_t = jnp.transpose(oa_s[:, :, :T], (2, 0, 1, 3)).reshape(Ms, MIX_A).astype(BF16)
        ob_t = jnp.transpose(ob_s, (1, 0, 2)).reshape(Ms, MIX_B).astype(BF16)
        xs, h2 = _out_proj(l, oa_t, ob_t, xs, mod_s[2], w_out_b,
                           vec3(norm2_g), mod_s[4], mod_s[3], tm=Ms, rows_per_group=Ms)
        hist_s = jnp.transpose(state_conv[l], (1, 0, 2)).reshape(1, CONV_W - 1, Bs, D_FF)
        act, tail = _up_proj(l, h2, w_up_b, conv_w, vec3(conv_b), hist_s, n_seq=Bs, steps=T, tn=TN_UP)
        xs = _down_proj(act, w_down_b, xs, mod_s[5], tm=Ms, tn=TN_DOWN, rows_per_group=Ms)
        outs[2].append(jnp.transpose(kn_s[:, :, :T], (0, 2, 1, 3)))
        outs[3].append(jnp.transpose(qkv[2], (0, 2, 1, 3)))
        outs[4].append(g_st)
        outs[5].append(jnp.transpose(tail.reshape(CONV_W - 1, Bs, D_FF), (1, 0, 2)))

    y_prompt = xp.reshape(B, S, D_MODEL)
    y_sample = jnp.transpose(xs.reshape(T, Bs, D_MODEL), (1, 0, 2))
    kv_prompt = tuple(jnp.transpose(a, (0, 1, 4, 2, 3)) for a in kv_p)
    return (y_prompt, y_sample) + kv_prompt + tuple(jnp.stack(o) for o in outs)
```

```python
import functools
import math

import numpy as np
import jax
import jax.numpy as jnp
from jax import lax
from jax.experimental import pallas as pl
from jax.experimental.pallas import tpu as pltpu

F32 = jnp.float32
BF16 = jnp.bfloat16

D_MODEL = 2048
DEPTH = 2
HEAD_DIM = 64
MIX_A = D_MODEL // 2
H_A = MIX_A // HEAD_DIM
DIL_GROUPS = ((128, 1), (512, 4), (2048, 16))
N_BUCKETS = 32
MAX_DISTANCE = 2048
MIX_B = D_MODEL - MIX_A
H_B = 4
DV_B = MIX_B // H_B
DK_B = DV_B // 2
KEY_B = H_B * DK_B
GATE_RANK = 16
GATE_TAU = 16.0
D_FF = ((8 * D_MODEL // 3 + 255) // 256) * 256
CONV_W = 3
EPS = 1e-6
ATTN_SCALE = HEAD_DIM ** -0.5
MAIN_COLS = 3 * MIX_A + 2 * KEY_B + 2 * MIX_B
BAND = 128
NEG = -1e30

LANES = 128
SUBLANES = 8
VMEM_LIMIT = 56 * 1024 * 1024
TM_IN, TM_OUT, TM_DOWN = 1024, 512, 1024
TN_IN, TN_UP, TN_DOWN = 1536, 512, 512
OUT_SPLIT = 2
UP_ROWS = 512
GLA_BLOCK = 128
GLA_SUB = 16
ATTN_PAR = 8
GLA_UNROLL = 4
GLA_PAR = 2

_NT = (((1,), (1,)), ((), ()))


def _cparams(*sem):
    return pltpu.CompilerParams(dimension_semantics=sem, vmem_limit_bytes=VMEM_LIMIT)


def _dot(a, b):
    return jnp.dot(a, b, preferred_element_type=F32)


def _dot_nt(a, b):
    return lax.dot_general(a, b, _NT, preferred_element_type=F32)


def _silu(x):
    return x * jax.nn.sigmoid(x)


def _rms(x, g):
    return x * lax.rsqrt(jnp.mean(x * x, axis=-1, keepdims=True) + EPS) * g


def _ada_kernel(c_ref, w_ref, b_ref, o_ref):
    a = _silu(c_ref[...]).astype(BF16)
    o_ref[...] = _dot(a, w_ref[...].astype(BF16)) + b_ref[...]


def _ada_mods(c_all, w_ada, b_ada):
    G = c_all.shape[0]
    tn = 1024
    return pl.pallas_call(
        _ada_kernel,
        grid=(DEPTH, 6 * D_MODEL // tn),
        in_specs=[pl.BlockSpec((G, D_MODEL), lambda l, j: (0, 0)),
                  pl.BlockSpec((None, D_MODEL, tn), lambda l, j: (l, 0, j)),
                  pl.BlockSpec((None, 1, tn), lambda l, j: (l, 0, j))],
        out_specs=pl.BlockSpec((None, G, tn), lambda l, j: (l, 0, j)),
        out_shape=jax.ShapeDtypeStruct((DEPTH, G, 6 * D_MODEL), F32),
        compiler_params=_cparams("parallel", "parallel"),
        name="ada_mods",
    )(c_all, w_ada, b_ada.reshape(DEPTH, 1, 6 * D_MODEL))


def _cast_kernel(x_ref, o_ref):
    o_ref[...] = x_ref[...].astype(BF16)


def _main_cols_bf16(w_in_t):
    tn = 1024
    return pl.pallas_call(
        _cast_kernel,
        grid=(DEPTH, MAIN_COLS // tn),
        in_specs=[pl.BlockSpec((None, tn, D_MODEL), lambda l, j: (l, j, 0))],
        out_specs=pl.BlockSpec((None, tn, D_MODEL), lambda l, j: (l, j, 0)),
        out_shape=jax.ShapeDtypeStruct((DEPTH, MAIN_COLS, D_MODEL), BF16),
        compiler_params=_cparams("parallel", "parallel"),
        name="w_in_bf16",
    )(w_in_t)


def _in_kernel(x_ref, g_ref, sc_ref, sh_ref, w_ref, cs_ref, wg_ref, wg2_ref, bg2_ref,
               o_ref, la_ref, h_scr):
    @pl.when(pl.program_id(1) == 0)
    def _():
        hb = (_rms(x_ref[...], g_ref[...]) * (1.0 + sc_ref[...]) + sh_ref[...]).astype(BF16)
        h_scr[...] = hb
        glr = _dot_nt(hb, wg_ref[...])
        z = _dot(glr.astype(BF16), wg2_ref[...]) + bg2_ref[...]
        la_ref[...] = (jnp.minimum(z, 0.0) - jnp.log(1.0 + jnp.exp(-jnp.abs(z)))) * (1.0 / GATE_TAU)

    o_ref[...] = _dot_nt(h_scr[...], w_ref[...]) * cs_ref[...]


def _in_proj(l, x, norm_g, scale, shift, w_main, col_scale, w_glr, w_g2, b_g2, tm, rows_per_group):
    M = x.shape[0]
    tn = TN_IN
    R = scale.shape[1]
    grp = lambda i, j: ((i * tm) // rows_per_group, 0, 0)
    return pl.pallas_call(
        _in_kernel,
        grid=(M // tm, MAIN_COLS // tn),
        in_specs=[pl.BlockSpec((tm, D_MODEL), lambda i, j: (i, 0)),
                  pl.BlockSpec((None, 1, D_MODEL), lambda i, j: (l, 0, 0)),
                  pl.BlockSpec((None, R, D_MODEL), grp),
                  pl.BlockSpec((None, R, D_MODEL), grp),
                  pl.BlockSpec((None, tn, D_MODEL), lambda i, j: (l, j, 0)),
                  pl.BlockSpec((1, tn), lambda i, j: (0, j)),
                  pl.BlockSpec((None, LANES, D_MODEL), lambda i, j: (l, 0, 0)),
                  pl.BlockSpec((None, LANES, KEY_B), lambda i, j: (l, 0, 0)),
                  pl.BlockSpec((None, 1, KEY_B), lambda i, j: (l, 0, 0))],
        out_specs=[pl.BlockSpec((tm, tn), lambda i, j: (i, j)),
                   pl.BlockSpec((tm, KEY_B), lambda i, j: (i, 0))],
        out_shape=[jax.ShapeDtypeStruct((M, MAIN_COLS), F32),
                   jax.ShapeDtypeStruct((M, KEY_B), F32)],
        scratch_shapes=[pltpu.VMEM((tm, D_MODEL), BF16)],
        compiler_params=_cparams("parallel", "arbitrary"),
        name="in_proj",
    )(x, norm_g, scale, shift, w_main, col_scale, w_glr, w_g2, b_g2)


def _out_kernel(oa_ref, ob_ref, x_ref, g1_ref, wa_ref, wb_ref, n2_ref, sc_ref, sh_ref, xo_ref, h2_ref):
    tm = x_ref.shape[0]
    halves = [pl.ds(c * (tm // OUT_SPLIT), tm // OUT_SPLIT) for c in range(OUT_SPLIT)]
    ys = [_dot(oa_ref[rs, :], wa_ref[...]) + _dot(ob_ref[rs, :], wb_ref[...]) for rs in halves]
    per_row = g1_ref.shape[0] == tm
    for rs, y in zip(halves, ys):
        mod = lambda ref: ref[rs, :] if per_row else ref[...]
        xn = x_ref[rs, :] + mod(g1_ref) * y
        xo_ref[rs, :] = xn
        h2_ref[rs, :] = (_rms(xn, n2_ref[...]) * (1.0 + mod(sc_ref)) + mod(sh_ref)).astype(BF16)


def _out_proj(l, oa, ob, x, gate1, w_out, norm2_g, scale2, shift2, tm, rows_per_group):
    M = x.shape[0]
    R = gate1.shape[1]
    grp = lambda i: ((i * tm) // rows_per_group, 0, 0)
    return pl.pallas_call(
        _out_kernel,
        grid=(M // tm,),
        in_specs=[pl.BlockSpec((tm, MIX_A), lambda i: (i, 0)),
                  pl.BlockSpec((tm, MIX_B), lambda i: (i, 0)),
                  pl.BlockSpec((tm, D_MODEL), lambda i: (i, 0)),
                  pl.BlockSpec((None, R, D_MODEL), grp),
                  pl.BlockSpec((None, MIX_A, D_MODEL), lambda i: (0, 0, 0)),
                  pl.BlockSpec((None, MIX_B, D_MODEL), lambda i: (0, 1, 0)),
                  pl.BlockSpec((None, 1, D_MODEL), lambda i: (l, 0, 0)),
                  pl.BlockSpec((None, R, D_MODEL), grp),
                  pl.BlockSpec((None, R, D_MODEL), grp)],
        out_specs=[pl.BlockSpec((tm, D_MODEL), lambda i: (i, 0)),
                   pl.BlockSpec((tm, D_MODEL), lambda i: (i, 0))],
        out_shape=[jax.ShapeDtypeStruct((M, D_MODEL), F32),
                   jax.ShapeDtypeStruct((M, D_MODEL), BF16)],
        compiler_params=_cparams("parallel"),
        name="out_proj",
    )(oa, ob, x, gate1, w_out, w_out, norm2_g, scale2, shift2)


def _up_kernel(h_ref, wg_ref, wu_ref, cw_ref, cb_ref, hist_ref, act_ref, tail_ref, *, n_seq, steps):
    rows = n_seq * steps
    tail = tail_ref.shape[0]
    tn = act_ref.shape[1]
    rc = min(rows, UP_ROWS)
    head = max(SUBLANES, (CONV_W - 1) * n_seq)
    assert rows % rc == 0 and rc % head == 0 and (rows == rc or n_seq == 1)
    rowh = lax.broadcasted_iota(jnp.int32, (head, tn), 0)
    cw = cw_ref[...]
    cb = cb_ref[...]

    def on_head(hist):
        if n_seq == 1:
            return jnp.broadcast_to(hist, (head, tn))
        return jnp.concatenate([hist] * (head // n_seq), axis=0)

    g_prev = None
    for c in range(rows // rc):
        h = h_ref[pl.ds(c * rc, rc), :]
        g = _dot(h, wg_ref[...])
        u = _dot(h, wu_ref[...])
        if c == 0:
            back1 = on_head(hist_ref[1])
            back2 = jnp.where(rowh < n_seq, on_head(hist_ref[0]), back1)
        else:
            back1 = pltpu.roll(g_prev[rc - head:], n_seq, 0)
            back2 = pltpu.roll(g_prev[rc - head:], 2 * n_seq, 0)
        r1 = pltpu.roll(g, n_seq, 0)
        r2 = pltpu.roll(g, 2 * n_seq, 0)
        g1 = jnp.concatenate([jnp.where(rowh < n_seq, back1, r1[:head]), r1[head:]], axis=0)
        g2 = jnp.concatenate([jnp.where(rowh < 2 * n_seq, back2, r2[:head]), r2[head:]], axis=0)
        conv = cb + cw[0:1] * g2 + cw[1:2] * g1 + cw[2:3] * g
        act_ref[pl.ds(c * rc, rc), :] = (_silu(conv) * u).astype(BF16)
        g_prev = g
    tail_ref[...] = g_prev[rc - tail:]


def _up_proj(l, h2, w_up, conv_w, conv_b, hist, n_seq, steps, tn):
    M = h2.shape[0]
    tm = n_seq * steps
    nj = D_FF // tn
    tail = max(SUBLANES, (CONV_W - 1) * n_seq)
    return pl.pallas_call(
        functools.partial(_up_kernel, n_seq=n_seq, steps=steps),
        grid=(M // tm, nj),
        in_specs=[pl.BlockSpec((tm, D_MODEL), lambda i, j: (i, 0)),
                  pl.BlockSpec((None, D_MODEL, tn), lambda i, j: (0, 0, j)),
                  pl.BlockSpec((None, D_MODEL, tn), lambda i, j: (0, 0, nj + j)),
                  pl.BlockSpec((None, CONV_W, tn), lambda i, j: (l, 0, j)),
                  pl.BlockSpec((None, 1, tn), lambda i, j: (l, 0, j)),
                  pl.BlockSpec((None, CONV_W - 1, n_seq, tn), lambda i, j: (i, 0, 0, j))],
        out_specs=[pl.BlockSpec((tm, tn), lambda i, j: (i, j)),
                   pl.BlockSpec((None, tail, tn), lambda i, j: (i, 0, j))],
        out_shape=[jax.ShapeDtypeStruct((M, D_FF), BF16),
                   jax.ShapeDtypeStruct((M // tm, tail, D_FF), F32)],
        compiler_params=_cparams("parallel", "parallel"),
        name="up_proj",
    )(h2, w_up, w_up, conv_w, conv_b, hist)


def _down_kernel(a_ref, w_ref, x_ref, g2_ref, o_ref):
    o_ref[...] = x_ref[...] + g2_ref[...] * _dot(a_ref[...], w_ref[...])


def _down_proj(act, w_down, x, gate2, tm, tn, rows_per_group):
    M = x.shape[0]
    R = gate2.shape[1]
    return pl.pallas_call(
        _down_kernel,
        grid=(M // tm, D_MODEL // tn),
        in_specs=[pl.BlockSpec((tm, D_FF), lambda i, j: (i, 0)),
                  pl.BlockSpec((None, D_FF, tn), lambda i, j: (0, 0, j)),
                  pl.BlockSpec((tm, tn), lambda i, j: (i, j)),
                  pl.BlockSpec((None, R, tn), lambda i, j: ((i * tm) // rows_per_group, 0, j))],
        out_specs=pl.BlockSpec((tm, tn), lambda i, j: (i, j)),
        out_shape=jax.ShapeDtypeStruct((M, D_MODEL), F32),
        compiler_params=_cparams("parallel", "parallel"),
        name="down_proj",
    )(act, w_down, x, gate2)


def _head_pair_norm(x, g, same_head):
    x2 = x * x
    hi = x2.astype(BF16)
    low = (x2 - hi.astype(F32)).astype(BF16)
    ms = (_dot(hi, same_head) + _dot(low, same_head)) * (1.0 / HEAD_DIM)
    return x * lax.rsqrt(ms + EPS) * g


def _attn_p_kernel(q_ref, k_ref, v_ref, bias_ref, qg_ref, kg_ref, og_ref, wo_ref, wu_ref, wd_ref, *rest,
                   seq, n_alias):
    (knt_ref, vt_ref, oa_ref, wo_out, wu_out, wd_out,
     qs_scr, ks_scr, perm_scr, m_scr, l_scr, acc_scr) = rest[n_alias:]
    for src, dst in ((wo_ref, wo_out), (wu_ref, wu_out), (wd_ref, wd_out)):
        dst[...] = src[...].astype(BF16)
    lo = lax.broadcasted_iota(jnp.int32, (1, LANES), 1) < HEAD_DIM
    head_of = lambda axis: lax.broadcasted_iota(jnp.int32, (LANES, LANES), axis) // HEAD_DIM
    same_head = jnp.where(head_of(0) == head_of(1), 1.0, 0.0).astype(BF16)
    kn = _head_pair_norm(k_ref[0], kg_ref[...], same_head)
    ks_scr[...] = kn
    kn_t = kn.T
    for ref, val_t in ((knt_ref, kn_t), (vt_ref, v_ref[0].T)):
        if n_alias:
            ref[0] = val_t.reshape(2, HEAD_DIM, seq)
        else:
            ref[0, 0] = val_t.reshape(2, HEAD_DIM, seq)
            ref[1:, 0] = jnp.zeros((ref.shape[0] - 1, 2, HEAD_DIM, seq), F32)
    qs_scr[...] = _head_pair_norm(q_ref[0], qg_ref[...], same_head) * ATTN_SCALE
    col = lax.broadcasted_iota(jnp.int32, (BAND, 2 * BAND), 1)

    stage_rows = [(lambda rs: qs_scr[rs, :], lambda rs: ks_scr[rs, :], lambda rs: v_ref[0, rs, :])]
    for g in range(1, len(DIL_GROUPS)):
        d_prev, d_cur = DIL_GROUPS[g - 1][1], DIL_GROUPS[g][1]
        ratio = d_cur // d_prev
        len_prev, len_cur = seq // d_prev, seq // d_cur
        for t in range(3):
            for c in range(d_prev):
                for m in range(ratio):
                    src = pl.ds(c * len_prev + m, len_cur, stride=ratio)
                    perm_scr[g - 1, t, pl.ds((c + d_prev * m) * len_cur, len_cur), :] = stage_rows[g - 1][t](src)
        stage_rows.append(tuple((lambda rs, g=g, t=t: perm_scr[g - 1, t, rs, :]) for t in range(3)))

    for g, (_, dil) in enumerate(DIL_GROUPS):
        n_blk = seq // dil // BAND
        has_prev = n_blk > 1
        q_rows, k_rows, v_rows = stage_rows[g]

        def rows(start, dil=dil):
            return pl.ds(start, BAND, stride=dil) if dil > 1 else pl.ds(start, BAND)

        def body(trip, carry, g=g, dil=dil, n_blk=n_blk, rows=rows, has_prev=has_prev,
                 q_rows=q_rows, k_rows=k_rows, v_rows=v_rows):
            blocks = []
            for u in range(ATTN_PAR):
                idx = trip * ATTN_PAR + u
                r = idx // n_blk
                nb = idx % n_blk
                cur = pl.ds(pl.multiple_of((r * n_blk + nb) * BAND, BAND), BAND)
                qb = q_rows(cur)
                kcat = k_rows(cur)
                vcat = v_rows(cur)
                bias = [bias_ref[g, h, :, BAND:] for h in range(2)]
                if has_prev:
                    prev = pl.ds(pl.multiple_of((r * n_blk + jnp.maximum(nb - 1, 0)) * BAND, BAND), BAND)
                    kcat = jnp.concatenate([k_rows(prev), kcat], axis=0)
                    vcat = jnp.concatenate([v_rows(prev), vcat], axis=0)
                    no_prev = jnp.logical_and(nb == 0, col < BAND)
                    bias = [jnp.where(no_prev, NEG, bias_ref[g, h]) for h in range(2)]
                blocks.append(dict(out=rows(r + nb * (BAND * dil)), q=qb, k=kcat.astype(BF16),
                                   v=vcat.astype(BF16), bias=bias))
            heads = [(blk, h) for blk in blocks for h in range(2)]
            s = [_dot_nt(jnp.where(lo if h == 0 else jnp.logical_not(lo), blk["q"], 0.0).astype(BF16), blk["k"])
                 + blk["bias"][h] for blk, h in heads]
            mx = [jnp.max(x, axis=-1, keepdims=True) for x in s]
            p = [jnp.exp(x - m) for x, m in zip(s, mx)]
            sm = [jnp.sum(x, axis=-1, keepdims=True) for x in p]
            o = [_dot(x.astype(BF16), blk["v"]) for x, (blk, _) in zip(p, heads)]
            for u, blk in enumerate(blocks):
                m_scr[g, blk["out"], :] = jnp.where(lo, mx[2 * u], mx[2 * u + 1])
                l_scr[g, blk["out"], :] = jnp.where(lo, sm[2 * u], sm[2 * u + 1])
                acc_scr[g, blk["out"], :] = jnp.where(lo, o[2 * u], o[2 * u + 1])
            return carry

        lax.fori_loop(0, seq // BAND // ATTN_PAR, body, 0)

    n_g = len(DIL_GROUPS)
    m_all = m_scr[0]
    for g in range(1, n_g):
        m_all = jnp.maximum(m_all, m_scr[g])
    num = jnp.zeros((seq, LANES), F32)
    den = jnp.zeros((seq, LANES), F32)
    for g in range(n_g):
        w = jnp.exp(m_scr[g] - m_all)
        num = num + w * acc_scr[g]
        den = den + w * l_scr[g]
    oa_ref[0] = _head_pair_norm(num / den, og_ref[...], same_head).astype(BF16)


def _attn_prompt(l, proj, bias_tiles, qg, kg, og, weights, stacked):
    B, S, _ = proj.shape
    assert bool(stacked) == (l > 0)
    n_pair = MIX_A // LANES
    n_g = len(DIL_GROUPS)
    blk = lambda off: pl.BlockSpec((1, S, LANES), lambda b, hp: (b, 0, off + hp))
    vec = pl.BlockSpec((1, LANES), lambda b, hp: (0, 0))
    if stacked:
        t_blk = pl.BlockSpec((None, 1, 2, HEAD_DIM, S), lambda b, hp: (l, b, hp, 0, 0))
    else:
        t_blk = pl.BlockSpec((DEPTH, 1, 2, HEAD_DIM, S), lambda b, hp: (0, b, hp, 0, 0))
    w_tile = lambda w: (w.shape[1] // B, w.shape[2] // n_pair)
    w_in_specs = [pl.BlockSpec((None,) + w_tile(w), lambda b, hp: (l, b, hp)) for w in weights]
    w_out_specs = [pl.BlockSpec((None,) + w_tile(w), lambda b, hp: (0, b, hp)) for w in weights]
    n_in = 7 + len(weights)
    return pl.pallas_call(
        functools.partial(_attn_p_kernel, seq=S, n_alias=len(stacked)),
        grid=(B, n_pair),
        in_specs=[blk(0), blk(n_pair), blk(2 * n_pair),
                  pl.BlockSpec((n_g, 2, BAND, 2 * BAND), lambda b, hp: (0, hp, 0, 0)),
                  vec, vec, vec] + w_in_specs + [pl.BlockSpec(memory_space=pl.ANY)] * len(stacked),
        out_specs=[t_blk, t_blk, pl.BlockSpec((1, S, LANES), lambda b, hp: (b, 0, hp))] + w_out_specs,
        out_shape=[jax.ShapeDtypeStruct((DEPTH, B, H_A, HEAD_DIM, S), F32),
                   jax.ShapeDtypeStruct((DEPTH, B, H_A, HEAD_DIM, S), F32),
                   jax.ShapeDtypeStruct((B, S, MIX_A), BF16)]
                  + [jax.ShapeDtypeStruct((1,) + w.shape[1:], BF16) for w in weights],
        input_output_aliases={n_in + i: i for i in range(len(stacked))},
        scratch_shapes=([pltpu.VMEM((S, LANES), F32)] * 2 + [pltpu.VMEM((n_g - 1, 3, S, LANES), F32)]
                        + [pltpu.VMEM((n_g, S, LANES), F32)] * 3),
        compiler_params=_cparams("parallel", "parallel"),
        name="attn_prompt",
    )(proj, proj, proj, bias_tiles, qg, kg, og, *weights, *stacked)


def _gla_p_kernel(q_ref, k_ref, v_ref, la_ref, r_ref, g_ref, ob_ref, st_ref, st_scr, *, seq):
    C = GLA_BLOCK
    n_sub = C // GLA_SUB
    st_scr[...] = jnp.zeros(st_scr.shape, F32)
    row = lax.broadcasted_iota(jnp.int32, (C, C), 0)
    tri = row >= lax.broadcasted_iota(jnp.int32, (C, C), 1)
    tri_b = jnp.where(tri, 1.0, 0.0).astype(BF16)
    krow = lax.broadcasted_iota(jnp.int32, (C, DK_B), 0)

    heads = range(GLA_PAR)
    ksl = [slice(DK_B * h, DK_B * (h + 1)) for h in heads]
    vsl = [slice(DV_B * h, DV_B * (h + 1)) for h in heads]

    def block(c, carry):
        rs = pl.ds(pl.multiple_of(c * C, C), C)
        la = [la_ref[0, rs, ksl[h]] for h in heads]
        hi = [x.astype(BF16) for x in la]
        r1 = [x - y.astype(F32) for x, y in zip(la, hi)]
        mid = [x.astype(BF16) for x in r1]
        low = [(x - y.astype(F32)).astype(BF16) for x, y in zip(r1, mid)]
        cum = [_dot(tri_b, a) + _dot(tri_b, b) + _dot(tri_b, d) for a, b, d in zip(hi, mid, low)]
        q = [q_ref[0, rs, ksl[h]] for h in heads]
        k = [k_ref[0, rs, ksl[h]] for h in heads]
        v = [v_ref[0, rs, vsl[h]] for h in heads]
        starts = [[(cum[h] - la[h])[GLA_SUB * i:GLA_SUB * i + 1, :] for i in range(n_sub)] for h in heads]
        start_rows = [jnp.concatenate([jnp.broadcast_to(s, (GLA_SUB, DK_B)) for s in starts[h]], axis=0)
                      for h in heads]
        qe = [(q[h] * jnp.exp(cum[h] - start_rows[h])).astype(BF16) for h in heads]
        khat = [[(k[h] * jnp.exp(jnp.where(krow < GLA_SUB * (i + 1), starts[h][i] - cum[h], NEG))).astype(BF16)
                 for i in range(n_sub)] for h in heads]
        att = [jnp.concatenate([_dot_nt(qe[h][GLA_SUB * i:GLA_SUB * (i + 1)], khat[h][i]) for i in range(n_sub)],
                               axis=0) for h in heads]
        att = [jnp.where(tri, x, 0.0).astype(BF16) for x in att]
        o = [_dot(att[h], v[h].astype(BF16))
             + _dot_nt((q[h] * jnp.exp(cum[h])).astype(BF16), st_scr[h].astype(BF16)) for h in heads]
        last = [x[C - 1:C, :] for x in cum]
        kd = [(k[h] * jnp.exp(last[h] - cum[h])).astype(BF16) for h in heads]
        upd = [_dot(v[h].T.astype(BF16), kd[h]) for h in heads]
        for h in heads:
            st_scr[h] = st_scr[h] * jnp.exp(last[h]) + upd[h]
            ob_ref[0, rs, vsl[h]] = (_rms(o[h], g_ref[...]) * _silu(r_ref[0, rs, vsl[h]])).astype(BF16)
        return carry

    lax.fori_loop(0, seq // C, block, 0, unroll=GLA_UNROLL)
    for h in heads:
        st_ref[0, h] = st_scr[h].T


def _gla_prompt(proj, log_a, gla_g):
    B, S, _ = proj.shape
    kw, vw = GLA_PAR * DK_B, GLA_PAR * DV_B
    q0 = 3 * MIX_A // kw
    k0 = q0 + KEY_B // kw
    v0 = (3 * MIX_A + 2 * KEY_B) // vw
    r0 = v0 + MIX_B // vw
    kblk = lambda off: pl.BlockSpec((1, S, kw), lambda b, h: (b, 0, off + h))
    vblk = lambda off: pl.BlockSpec((1, S, vw), lambda b, h: (b, 0, off + h))
    return pl.pallas_call(
        functools.partial(_gla_p_kernel, seq=S),
        grid=(B, H_B // GLA_PAR),
        in_specs=[kblk(q0), kblk(k0), vblk(v0), kblk(0), vblk(r0),
                  pl.BlockSpec((1, DV_B), lambda b, h: (0, 0))],
        out_specs=[vblk(0), pl.BlockSpec((1, GLA_PAR, DK_B, DV_B), lambda b, h: (b, h, 0, 0))],
        out_shape=[jax.ShapeDtypeStruct((B, S, MIX_B), BF16),
                   jax.ShapeDtypeStruct((B, H_B, DK_B, DV_B), F32)],
        scratch_shapes=[pltpu.VMEM((GLA_PAR, DV_B, DK_B), F32)],
        compiler_params=_cparams("parallel", "parallel"),
        name="gla_prompt",
    )(proj, proj, proj, log_a, proj, gla_g)


def _gla_s_kernel(q_ref, k_ref, v_ref, la_ref, r_ref, g_ref, s_ref, ob_ref, so_ref, *, steps):
    rowi = lax.broadcasted_iota(jnp.int32, (DK_B, DK_B), 0)
    heads = range(H_B)
    ks = [slice(DK_B * h, DK_B * (h + 1)) for h in heads]
    vs = [slice(DV_B * h, DV_B * (h + 1)) for h in heads]
    x = [jnp.zeros((DK_B, DK_B), F32) for _ in heads]
    for t in range(steps):
        x = [jnp.where(rowi == t, k_ref[0, t:t + 1, ks[h]], x[h]) for h in heads]
        x = [jnp.where(rowi == steps + t, jnp.exp(la_ref[0, t:t + 1, ks[h]]), x[h]) for h in heads]
        x = [jnp.where(rowi == 2 * steps + t, q_ref[0, t:t + 1, ks[h]], x[h]) for h in heads]
    xt = [a.T for a in x]
    st = [s_ref[0, h] for h in heads]
    for t in range(steps):
        st = [xt[h][:, steps + t:steps + t + 1] * st[h] + xt[h][:, t:t + 1] * v_ref[0, t:t + 1, vs[h]]
              for h in heads]
        o = [jnp.sum(xt[h][:, 2 * steps + t:2 * steps + t + 1] * st[h], axis=0, keepdims=True) for h in heads]
        for h in heads:
            ob_ref[0, t:t + 1, vs[h]] = _rms(o[h], g_ref[...]) * _silu(r_ref[0, t:t + 1, vs[h]])
    for h in heads:
        so_ref[0, h] = st[h]


def _gla_sample(l, q, k, v, log_a, r, gla_g, state):
    B, T, _ = q.shape
    kb = pl.BlockSpec((1, T, KEY_B), lambda b: (b, 0, 0))
    vb = pl.BlockSpec((1, T, MIX_B), lambda b: (b, 0, 0))
    return pl.pallas_call(
        functools.partial(_gla_s_kernel, steps=T),
        grid=(B,),
        in_specs=[kb, kb, vb, kb, vb, pl.BlockSpec((1, DV_B), lambda b: (0, 0)),
                  pl.BlockSpec((None, 1, H_B, DK_B, DV_B), lambda b: (l, b, 0, 0, 0))],
        out_specs=[vb, pl.BlockSpec((1, H_B, DK_B, DV_B), lambda b: (b, 0, 0, 0))],
        out_shape=[jax.ShapeDtypeStruct((B, T, MIX_B), F32),
                   jax.ShapeDtypeStruct((B, H_B, DK_B, DV_B), F32)],
        compiler_params=_cparams("parallel"),
        name="gla_sample",
    )(q, k, v, log_a, r, gla_g, state)


def _attn_s_kernel(q_ref, k_ref, v_ref, kt_ref, vt_ref, bnear_ref, bfar_ref, bnew_ref,
                   qg_ref, kg_ref, og_ref, kn_ref, oa_ref, kpad_scr, vpad_scr, *, n_near):
    n_head, n_row, _ = kn_ref.shape[1:]
    width = kt_ref.shape[-1]
    kn = _rms(k_ref[0], kg_ref[...])
    kn_ref[0] = kn
    qs = _rms(q_ref[0], qg_ref[...]) * ATTN_SCALE
    kpad_scr[...] = jnp.zeros(kpad_scr.shape, F32)
    vpad_scr[...] = jnp.zeros(vpad_scr.shape, F32)
    kpad_scr[:, 0:n_row, :] = kn
    vpad_scr[:, 0:n_row, :] = v_ref[0]
    n_g = len(DIL_GROUPS)
    for h in range(n_head):
        qh = qs[h].astype(BF16)
        s = _dot(qh, kt_ref[h].astype(BF16))
        s_new = _dot_nt(qh, kpad_scr[h].astype(BF16))
        far = s + bfar_ref[h]
        s_near = s[:, width - n_near:]
        near = [s_near + bnear_ref[g, h] for g in range(n_g - 1)]
        new = [s_new + bnew_ref[g, h] for g in range(n_g)]
        m = jnp.max(far, axis=-1, keepdims=True)
        for x in near + new:
            m = jnp.maximum(m, jnp.max(x, axis=-1, keepdims=True))
        p_far = jnp.exp(far - m)
        p_near = sum(jnp.exp(x - m) for x in near)
        p_new = sum(jnp.exp(x - m) for x in new)
        den = (jnp.sum(p_far, axis=-1, keepdims=True) + jnp.sum(p_near, axis=-1, keepdims=True)
               + jnp.sum(p_new, axis=-1, keepdims=True))
        p = jnp.concatenate([p_far[:, :width - n_near], p_far[:, width - n_near:] + p_near], axis=1)
        o = _dot_nt(p.astype(BF16), vt_ref[h].astype(BF16)) + _dot(p_new.astype(BF16), vpad_scr[h].astype(BF16))
        oa_ref[0, h] = _rms(o / den, og_ref[...])


def _attn_sample(l, qkv, cache_kt, cache_vt, bias_near, bias_far, bias_new, qg, kg, og):
    _, B, n_head, n_row, _ = qkv.shape
    W = cache_kt.shape[-1]
    n_near = bias_near.shape[-1]
    qblk = lambda i: pl.BlockSpec((None, 1, n_head, n_row, HEAD_DIM), lambda b: (i, b, 0, 0, 0))
    cblk = pl.BlockSpec((None, None, n_head, HEAD_DIM, W), lambda b: (l, b, 0, 0, 0))
    full = lambda a: pl.BlockSpec(a.shape, lambda b: (0,) * a.ndim)
    vec = pl.BlockSpec((1, HEAD_DIM), lambda b: (0, 0))
    out_blk = pl.BlockSpec((1, n_head, n_row, HEAD_DIM), lambda b: (b, 0, 0, 0))
    return pl.pallas_call(
        functools.partial(_attn_s_kernel, n_near=n_near),
        grid=(B,),
        in_specs=[qblk(0), qblk(1), qblk(2), cblk, cblk,
                  full(bias_near), full(bias_far), full(bias_new), vec, vec, vec],
        out_specs=[out_blk, out_blk],
        out_shape=[jax.ShapeDtypeStruct((B, n_head, n_row, HEAD_DIM), F32),
                   jax.ShapeDtypeStruct((B, n_head, n_row, HEAD_DIM), F32)],
        scratch_shapes=[pltpu.VMEM((n_head, BAND, HEAD_DIM), F32)] * 2,
        compiler_params=_cparams("parallel"),
        name="attn_sample",
    )(qkv, qkv, qkv, cache_kt, cache_vt, bias_near, bias_far, bias_new, qg, kg, og)


def _bucket_table(n):
    dist = np.arange(n)
    exact = N_BUCKETS // 2
    df = np.maximum(dist, 1).astype(np.float32)
    large = exact + (np.log(df / np.float32(exact)) / np.float32(math.log(MAX_DISTANCE / exact))
                     * np.float32(N_BUCKETS - exact)).astype(np.int32)
    return np.where(dist < exact, dist, np.minimum(large, N_BUCKETS - 1))


def _prompt_bias_tiles(rel_bias):
    span = 3 * BAND
    tiles = []
    for _, dil in DIL_GROUPS:
        steps = np.arange(BAND, -1, -1)
        vec = rel_bias[_bucket_table(BAND * dil + 1)[steps * dil]].T.astype(F32)
        u = jnp.concatenate([vec, jnp.full((H_A, span - BAND - 1), NEG, F32)], axis=1)
        t = jnp.tile(u, (1, BAND))[:, :BAND * (span - 1)].reshape(H_A, BAND, span - 1)
        tiles.append(t[:, :, :2 * BAND])
    return jnp.stack(tiles)


def _sample_bias_tiles(rel_bias, steps, rows, wbuf):
    n_dist = wbuf + rows
    bucket = _bucket_table(n_dist)
    dist = np.arange(n_dist)
    tab = rel_bias[bucket].T.astype(F32)
    n_near = DIL_GROUPS[-2][0]
    tiles = []
    for win, dil in DIL_GROUPS:
        ok = (dist % dil == 0) & (dist >= dil) & (dist <= win)
        rev = jnp.where(ok[None, ::-1], tab[:, ::-1], NEG)
        tiles.append(jnp.stack([rev[:, rows - 1 - t:rows - 1 - t + wbuf] for t in range(rows)], axis=1))
    near = jnp.stack([t[:, :, wbuf - n_near:] for t in tiles[:-1]])
    far = tiles[-1]
    t_new = np.arange(rows)[:, None]
    tau = np.arange(BAND)[None, :]
    live = (t_new < steps) & (tau < steps)
    back = np.clip(t_new - tau, 0, steps)
    new = []
    for g, (_, dil) in enumerate(DIL_GROUPS):
        ok = live & ((tau <= t_new) if g == 0 else (tau == t_new))
        new.append(jnp.where(ok[None], rel_bias[bucket[back * dil]].transpose(2, 0, 1).astype(F32), NEG))
    return near, far, jnp.stack(new)


def kernel(x_prompt, x_sample, cache_k_win, cache_v_win, state_gla, state_conv, c_prompt, c_sample,
           rel_bias, norm1_g, norm2_g, w_ada, b_ada, w_in, q_norm_g, k_norm_g, w_gate2, b_gate2,
           a_out_g, gla_out_g, w_out, w_up, conv_w, conv_b, w_down):
    B, S, _ = x_prompt.shape
    Bs, T, _ = x_sample.shape
    wbuf = cache_k_win.shape[2]
    Mp, Ms = B * S, Bs * T
    assert wbuf == DIL_GROUPS[-1][0] and T <= SUBLANES

    w_in_t = jnp.transpose(w_in, (0, 2, 1))
    w_main = _main_cols_bf16(w_in_t)
    w_glr = jnp.pad(w_in_t[:, MAIN_COLS:], ((0, 0), (0, LANES - GATE_RANK), (0, 0))).astype(BF16)
    w_g2 = jnp.pad(w_gate2, ((0, 0), (0, LANES - GATE_RANK), (0, 0))).astype(BF16)
    b_g2 = b_gate2.reshape(DEPTH, 1, KEY_B)
    col_scale = jnp.ones((1, MAIN_COLS), F32).at[:, 3 * MIX_A:3 * MIX_A + KEY_B].set(DK_B ** -0.5)
    vec3 = lambda a: a.reshape(DEPTH, 1, -1)
    pair = lambda a, l: jnp.tile(a[l], LANES // HEAD_DIM).reshape(1, LANES)
    head = lambda a, l: a[l].reshape(1, HEAD_DIM)

    mods = _ada_mods(jnp.concatenate([c_prompt, c_sample], axis=0), w_ada, b_ada)

    bias_p = _prompt_bias_tiles(rel_bias)
    bias_near, bias_far, bias_new = _sample_bias_tiles(rel_bias, T, SUBLANES, wbuf)

    cache_kt = jnp.transpose(cache_k_win, (0, 1, 3, 4, 2))
    cache_vt = jnp.transpose(cache_v_win, (0, 1, 3, 4, 2))
    hist_p = jnp.zeros((B, CONV_W - 1, 1, D_FF), F32)

    xp = x_prompt.reshape(Mp, D_MODEL)
    xs = jnp.transpose(x_sample, (1, 0, 2)).reshape(Ms, D_MODEL)
    outs = [[] for _ in range(6)]
    kv_p = ()
    for l in range(DEPTH):
        mod_p = [m.reshape(B, 1, D_MODEL) for m in jnp.split(mods[l, :B], 6, axis=-1)]
        mod_s = [jnp.tile(m, (T, 1)).reshape(1, Ms, D_MODEL) for m in jnp.split(mods[l, B:], 6, axis=-1)]

        proj, log_a = _in_proj(l, xp, vec3(norm1_g), mod_p[1], mod_p[0], w_main, col_scale, w_glr, w_g2, b_g2,
                               tm=TM_IN, rows_per_group=S)
        proj = proj.reshape(B, S, MAIN_COLS)
        knt, vt, oa, w_out_b, w_up_b, w_down_b = _attn_prompt(
            l, proj, bias_p, pair(q_norm_g, l), pair(k_norm_g, l), pair(a_out_g, l), (w_out, w_up, w_down), kv_p)
        kv_p = (knt, vt)
        ob, g_st = _gla_prompt(proj, log_a.reshape(B, S, KEY_B), gla_out_g[l].reshape(1, DV_B))
        xp, h2 = _out_proj(l, oa.reshape(Mp, MIX_A), ob.reshape(Mp, MIX_B), xp, mod_p[2], w_out_b,
                           vec3(norm2_g), mod_p[4], mod_p[3], tm=TM_OUT, rows_per_group=S)
        act, tail = _up_proj(l, h2, w_up_b, conv_w, vec3(conv_b), hist_p, n_seq=1, steps=S, tn=TN_UP)
        xp = _down_proj(act, w_down_b, xp, mod_p[5], tm=TM_DOWN, tn=TN_DOWN, rows_per_group=S)
        outs[0].append(g_st)
        outs[1].append(tail[:, tail.shape[1] - (CONV_W - 1):])

        proj, log_a = _in_proj(l, xs, vec3(norm1_g), mod_s[1], mod_s[0], w_main, col_scale, w_glr, w_g2, b_g2,
                               tm=Ms, rows_per_group=Ms)
        bm = lambda a: jnp.transpose(a.reshape(T, Bs, -1), (1, 0, 2))
        qkv = jnp.transpose(proj[:, :3 * MIX_A].reshape(T, Bs, 3, H_A, HEAD_DIM), (2, 1, 3, 0, 4))
        qkv_pad = jnp.pad(qkv, ((0, 0), (0, 0), (0, 0), (0, SUBLANES - T), (0, 0)))
        kn_s, oa_s = _attn_sample(l, qkv_pad, cache_kt, cache_vt, bias_near, bias_far, bias_new,
                                  head(q_norm_g, l), head(k_norm_g, l), head(a_out_g, l))
        gla_in = bm(proj[:, 3 * MIX_A:])
        ob_s, g_st = _gla_sample(l, gla_in[..., :KEY_B], gla_in[..., KEY_B:2 * KEY_B],
                                 gla_in[..., 2 * KEY_B:2 * KEY_B + MIX_B], bm(log_a),
                                 gla_in[..., 2 * KEY_B + MIX_B:], gla_out_g[l].reshape(1, DV_B), state_gla)
        oa_t = jnp.transpose(oa_s[:, :, :T], (2, 0, 1, 3)).reshape(Ms, MIX_A).astype(BF16)
        ob_t = jnp.transpose(ob_s, (1, 0, 2)).reshape(Ms, MIX_B).astype(BF16)
        xs, h2 = _out_proj(l, oa_t, ob_t, xs, mod_s[2], w_out_b,
                           vec3(norm2_g), mod_s[4], mod_s[3], tm=Ms, rows_per_group=Ms)
        hist_s = jnp.transpose(state_conv[l], (1, 0, 2)).reshape(1, CONV_W - 1, Bs, D_FF)
        act, tail = _up_proj(l, h2, w_up_b, conv_w, vec3(conv_b), hist_s, n_seq=Bs, steps=T, tn=TN_UP)
        xs = _down_proj(act, w_down_b, xs, mod_s[5], tm=Ms, tn=TN_DOWN, rows_per_group=Ms)
        outs[2].append(jnp.transpose(kn_s[:, :, :T], (0, 2, 1, 3)))
        outs[3].append(jnp.transpose(qkv[2], (0, 2, 1, 3)))
        outs[4].append(g_st)
        outs[5].append(jnp.transpose(tail.reshape(CONV_W - 1, Bs, D_FF), (1, 0, 2)))

    y_prompt = xp.reshape(B, S, D_MODEL)
    y_sample = jnp.transpose(xs.reshape(T, Bs, D_MODEL), (1, 0, 2))
    kv_prompt = tuple(jnp.transpose(a, (0, 1, 4, 2, 3)) for a in kv_p)
    return (y_prompt, y_sample) + kv_prompt + tuple(jnp.stack(o) for o in outs)
```

```python
import functools
import math

import numpy as np
import jax
import jax.numpy as jnp
from jax import lax
from jax.experimental import pallas as pl
from jax.experimental.pallas import tpu as pltpu

F32 = jnp.float32
BF16 = jnp.bfloat16

D_MODEL = 2048
DEPTH = 2
HEAD_DIM = 64
MIX_A = D_MODEL // 2
H_A = MIX_A // HEAD_DIM
DIL_GROUPS = ((128, 1), (512, 4), (2048, 16))
N_BUCKETS = 32
MAX_DISTANCE = 2048
MIX_B = D_MODEL - MIX_A
H_B = 4
DV_B = MIX_B // H_B
DK_B = DV_B // 2
KEY_B = H_B * DK_B
GATE_RANK = 16
GATE_TAU = 16.0
D_FF = ((8 * D_MODEL // 3 + 255) // 256) * 256
CONV_W = 3
EPS = 1e-6
ATTN_SCALE = HEAD_DIM ** -0.5
MAIN_COLS = 3 * MIX_A + 2 * KEY_B + 2 * MIX_B
BAND = 128
NEG = -1e30

LANES = 128
SUBLANES = 8
VMEM_LIMIT = 56 * 1024 * 1024
TM_IN, TM_OUT, TM_DOWN = 1024, 512, 1024
TN_IN, TN_UP, TN_DOWN = 1536, 512, 512
OUT_SPLIT = 2
UP_ROWS = 512
ATTN_S_PAR = 4
GLA_BLOCK = 128
GLA_SUB = 16
ATTN_PAR = 8
GLA_UNROLL = 4
GLA_PAR = 2

_NT = (((1,), (1,)), ((), ()))


def _cparams(*sem):
    return pltpu.CompilerParams(dimension_semantics=sem, vmem_limit_bytes=VMEM_LIMIT)


def _dot(a, b):
    return jnp.dot(a, b, preferred_element_type=F32)


def _dot_nt(a, b):
    return lax.dot_general(a, b, _NT, preferred_element_type=F32)


def _silu(x):
    return x * jax.nn.sigmoid(x)


def _rms(x, g):
    return x * lax.rsqrt(jnp.mean(x * x, axis=-1, keepdims=True) + EPS) * g


def _ada_kernel(c_ref, w_ref, b_ref, o_ref):
    a = _silu(c_ref[...]).astype(BF16)
    o_ref[...] = _dot(a, w_ref[...].astype(BF16)) + b_ref[...]


def _ada_mods(c_all, w_ada, b_ada):
    G = c_all.shape[0]
    tn = 1024
    return pl.pallas_call(
        _ada_kernel,
        grid=(DEPTH, 6 * D_MODEL // tn),
        in_specs=[pl.BlockSpec((G, D_MODEL), lambda l, j: (0, 0)),
                  pl.BlockSpec((None, D_MODEL, tn), lambda l, j: (l, 0, j)),
                  pl.BlockSpec((None, 1, tn), lambda l, j: (l, 0, j))],
        out_specs=pl.BlockSpec((None, G, tn), lambda l, j: (l, 0, j)),
        out_shape=jax.ShapeDtypeStruct((DEPTH, G, 6 * D_MODEL), F32),
        compiler_params=_cparams("parallel", "parallel"),
        name="ada_mods",
    )(c_all, w_ada, b_ada.reshape(DEPTH, 1, 6 * D_MODEL))


def _cast_kernel(x_ref, o_ref):
    o_ref[...] = x_ref[...].astype(BF16)


def _main_cols_bf16(w_in_t):
    tn = 1024
    return pl.pallas_call(
        _cast_kernel,
        grid=(DEPTH, MAIN_COLS // tn),
        in_specs=[pl.BlockSpec((None, tn, D_MODEL), lambda l, j: (l, j, 0))],
        out_specs=pl.BlockSpec((None, tn, D_MODEL), lambda l, j: (l, j, 0)),
        out_shape=jax.ShapeDtypeStruct((DEPTH, MAIN_COLS, D_MODEL), BF16),
        compiler_params=_cparams("parallel", "parallel"),
        name="w_in_bf16",
    )(w_in_t)


def _in_kernel(x_ref, g_ref, sc_ref, sh_ref, w_ref, cs_ref, wg_ref, wg2_ref, bg2_ref,
               o_ref, la_ref, h_scr):
    @pl.when(pl.program_id(1) == 0)
    def _():
        hb = (_rms(x_ref[...], g_ref[...]) * (1.0 + sc_ref[...]) + sh_ref[...]).astype(BF16)
        h_scr[...] = hb
        glr = _dot_nt(hb, wg_ref[...])
        z = _dot(glr.astype(BF16), wg2_ref[...]) + bg2_ref[...]
        la_ref[...] = (jnp.minimum(z, 0.0) - jnp.log(1.0 + jnp.exp(-jnp.abs(z)))) * (1.0 / GATE_TAU)

    o_ref[...] = _dot_nt(h_scr[...], w_ref[...]) * cs_ref[...]


def _in_proj(l, x, norm_g, scale, shift, w_main, col_scale, w_glr, w_g2, b_g2, tm, rows_per_group):
    M = x.shape[0]
    tn = TN_IN
    R = scale.shape[1]
    grp = lambda i, j: ((i * tm) // rows_per_group, 0, 0)
    return pl.pallas_call(
        _in_kernel,
        grid=(M // tm, MAIN_COLS // tn),
        in_specs=[pl.BlockSpec((tm, D_MODEL), lambda i, j: (i, 0)),
                  pl.BlockSpec((None, 1, D_MODEL), lambda i, j: (l, 0, 0)),
                  pl.BlockSpec((None, R, D_MODEL), grp),
                  pl.BlockSpec((None, R, D_MODEL), grp),
                  pl.BlockSpec((None, tn, D_MODEL), lambda i, j: (l, j, 0)),
                  pl.BlockSpec((1, tn), lambda i, j: (0, j)),
                  pl.BlockSpec((None, LANES, D_MODEL), lambda i, j: (l, 0, 0)),
                  pl.BlockSpec((None, LANES, KEY_B), lambda i, j: (l, 0, 0)),
                  pl.BlockSpec((None, 1, KEY_B), lambda i, j: (l, 0, 0))],
        out_specs=[pl.BlockSpec((tm, tn), lambda i, j: (i, j)),
                   pl.BlockSpec((tm, KEY_B), lambda i, j: (i, 0))],
        out_shape=[jax.ShapeDtypeStruct((M, MAIN_COLS), F32),
                   jax.ShapeDtypeStruct((M, KEY_B), F32)],
        scratch_shapes=[pltpu.VMEM((tm, D_MODEL), BF16)],
        compiler_params=_cparams("parallel", "arbitrary"),
        name="in_proj",
    )(x, norm_g, scale, shift, w_main, col_scale, w_glr, w_g2, b_g2)


def _out_kernel(oa_ref, ob_ref, x_ref, g1_ref, wa_ref, wb_ref, n2_ref, sc_ref, sh_ref, xo_ref, h2_ref):
    tm = x_ref.shape[0]
    halves = [pl.ds(c * (tm // OUT_SPLIT), tm // OUT_SPLIT) for c in range(OUT_SPLIT)]
    ys = [_dot(oa_ref[rs, :], wa_ref[...]) + _dot(ob_ref[rs, :], wb_ref[...]) for rs in halves]
    per_row = g1_ref.shape[0] == tm
    for rs, y in zip(halves, ys):
        mod = lambda ref: ref[rs, :] if per_row else ref[...]
        xn = x_ref[rs, :] + mod(g1_ref) * y
        xo_ref[rs, :] = xn
        h2_ref[rs, :] = (_rms(xn, n2_ref[...]) * (1.0 + mod(sc_ref)) + mod(sh_ref)).astype(BF16)


def _out_proj(l, oa, ob, x, gate1, w_out, norm2_g, scale2, shift2, tm, rows_per_group):
    M = x.shape[0]
    R = gate1.shape[1]
    grp = lambda i: ((i * tm) // rows_per_group, 0, 0)
    return pl.pallas_call(
        _out_kernel,
        grid=(M // tm,),
        in_specs=[pl.BlockSpec((tm, MIX_A), lambda i: (i, 0)),
                  pl.BlockSpec((tm, MIX_B), lambda i: (i, 0)),
                  pl.BlockSpec((tm, D_MODEL), lambda i: (i, 0)),
                  pl.BlockSpec((None, R, D_MODEL), grp),
                  pl.BlockSpec((None, MIX_A, D_MODEL), lambda i: (0, 0, 0)),
                  pl.BlockSpec((None, MIX_B, D_MODEL), lambda i: (0, 1, 0)),
                  pl.BlockSpec((None, 1, D_MODEL), lambda i: (l, 0, 0)),
                  pl.BlockSpec((None, R, D_MODEL), grp),
                  pl.BlockSpec((None, R, D_MODEL), grp)],
        out_specs=[pl.BlockSpec((tm, D_MODEL), lambda i: (i, 0)),
                   pl.BlockSpec((tm, D_MODEL), lambda i: (i, 0))],
        out_shape=[jax.ShapeDtypeStruct((M, D_MODEL), F32),
                   jax.ShapeDtypeStruct((M, D_MODEL), BF16)],
        compiler_params=_cparams("parallel"),
        name="out_proj",
    )(oa, ob, x, gate1, w_out, w_out, norm2_g, scale2, shift2)


def _up_kernel(h_ref, wg_ref, wu_ref, cw_ref, cb_ref, hist_ref, act_ref, tail_ref, *, n_seq, steps):
    rows = n_seq * steps
    tail = tail_ref.shape[0]
    tn = act_ref.shape[1]
    rc = min(rows, UP_ROWS)
    head = max(SUBLANES, (CONV_W - 1) * n_seq)
    assert rows % rc == 0 and rc % head == 0 and (rows == rc or n_seq == 1)
    rowh = lax.broadcasted_iota(jnp.int32, (head, tn), 0)
    cw = cw_ref[...]
    cb = cb_ref[...]

    def on_head(hist):
        if n_seq == 1:
            return jnp.broadcast_to(hist, (head, tn))
        return jnp.concatenate([hist] * (head // n_seq), axis=0)

    g_prev = None
    for c in range(rows // rc):
        h = h_ref[pl.ds(c * rc, rc), :]
        g = _dot(h, wg_ref[...])
        u = _dot(h, wu_ref[...])
        if c == 0:
            back1 = on_head(hist_ref[1])
            back2 = jnp.where(rowh < n_seq, on_head(hist_ref[0]), back1)
        else:
            back1 = pltpu.roll(g_prev[rc - head:], n_seq, 0)
            back2 = pltpu.roll(g_prev[rc - head:], 2 * n_seq, 0)
        r1 = pltpu.roll(g, n_seq, 0)
        r2 = pltpu.roll(g, 2 * n_seq, 0)
        g1 = jnp.concatenate([jnp.where(rowh < n_seq, back1, r1[:head]), r1[head:]], axis=0)
        g2 = jnp.concatenate([jnp.where(rowh < 2 * n_seq, back2, r2[:head]), r2[head:]], axis=0)
        conv = cb + cw[0:1] * g2 + cw[1:2] * g1 + cw[2:3] * g
        act_ref[pl.ds(c * rc, rc), :] = (_silu(conv) * u).astype(BF16)
        g_prev = g
    tail_ref[...] = g_prev[rc - tail:]


def _up_proj(l, h2, w_up, conv_w, conv_b, hist, n_seq, steps, tn):
    M = h2.shape[0]
    tm = n_seq * steps
    nj = D_FF // tn
    tail = max(SUBLANES, (CONV_W - 1) * n_seq)
    return pl.pallas_call(
        functools.partial(_up_kernel, n_seq=n_seq, steps=steps),
        grid=(M // tm, nj),
        in_specs=[pl.BlockSpec((tm, D_MODEL), lambda i, j: (i, 0)),
                  pl.BlockSpec((None, D_MODEL, tn), lambda i, j: (0, 0, j)),
                  pl.BlockSpec((None, D_MODEL, tn), lambda i, j: (0, 0, nj + j)),
                  pl.BlockSpec((None, CONV_W, tn), lambda i, j: (l, 0, j)),
                  pl.BlockSpec((None, 1, tn), lambda i, j: (l, 0, j)),
                  pl.BlockSpec((None, CONV_W - 1, n_seq, tn), lambda i, j: (i, 0, 0, j))],
        out_specs=[pl.BlockSpec((tm, tn), lambda i, j: (i, j)),
                   pl.BlockSpec((None, tail, tn), lambda i, j: (i, 0, j))],
        out_shape=[jax.ShapeDtypeStruct((M, D_FF), BF16),
                   jax.ShapeDtypeStruct((M // tm, tail, D_FF), F32)],
        compiler_params=_cparams("parallel", "parallel"),
        name="up_proj",
    )(h2, w_up, w_up, conv_w, conv_b, hist)


def _down_kernel(a_ref, w_ref, x_ref, g2_ref, o_ref):
    o_ref[...] = x_ref[...] + g2_ref[...] * _dot(a_ref[...], w_ref[...])


def _down_proj(act, w_down, x, gate2, tm, tn, rows_per_group):
    M = x.shape[0]
    R = gate2.shape[1]
    return pl.pallas_call(
        _down_kernel,
        grid=(M // tm, D_MODEL // tn),
        in_specs=[pl.BlockSpec((tm, D_FF), lambda i, j: (i, 0)),
                  pl.BlockSpec((None, D_FF, tn), lambda i, j: (0, 0, j)),
                  pl.BlockSpec((tm, tn), lambda i, j: (i, j)),
                  pl.BlockSpec((None, R, tn), lambda i, j: ((i * tm) // rows_per_group, 0, j))],
        out_specs=pl.BlockSpec((tm, tn), lambda i, j: (i, j)),
        out_shape=jax.ShapeDtypeStruct((M, D_MODEL), F32),
        compiler_params=_cparams("parallel", "parallel"),
        name="down_proj",
    )(act, w_down, x, gate2)


def _head_pair_norm(x, g, same_head):
    x2 = x * x
    hi = x2.astype(BF16)
    low = (x2 - hi.astype(F32)).astype(BF16)
    ms = (_dot(hi, same_head) + _dot(low, same_head)) * (1.0 / HEAD_DIM)
    return x * lax.rsqrt(ms + EPS) * g


def _attn_p_kernel(q_ref, k_ref, v_ref, bias_ref, qg_ref, kg_ref, og_ref, wo_ref, wu_ref, wd_ref, *rest,
                   seq, n_alias):
    (knt_ref, vt_ref, oa_ref, wo_out, wu_out, wd_out,
     qs_scr, ks_scr, perm_scr, m_scr, l_scr, acc_scr) = rest[n_alias:]
    for src, dst in ((wo_ref, wo_out), (wu_ref, wu_out), (wd_ref, wd_out)):
        dst[...] = src[...].astype(BF16)
    lo = lax.broadcasted_iota(jnp.int32, (1, LANES), 1) < HEAD_DIM
    head_of = lambda axis: lax.broadcasted_iota(jnp.int32, (LANES, LANES), axis) // HEAD_DIM
    same_head = jnp.where(head_of(0) == head_of(1), 1.0, 0.0).astype(BF16)
    kn = _head_pair_norm(k_ref[0], kg_ref[...], same_head)
    ks_scr[...] = kn
    kn_t = kn.T
    for ref, val_t in ((knt_ref, kn_t), (vt_ref, v_ref[0].T)):
        if n_alias:
            ref[0] = val_t.reshape(2, HEAD_DIM, seq)
        else:
            ref[0, 0] = val_t.reshape(2, HEAD_DIM, seq)
            ref[1:, 0] = jnp.zeros((ref.shape[0] - 1, 2, HEAD_DIM, seq), F32)
    qs_scr[...] = _head_pair_norm(q_ref[0], qg_ref[...], same_head) * ATTN_SCALE
    col = lax.broadcasted_iota(jnp.int32, (BAND, 2 * BAND), 1)

    stage_rows = [(lambda rs: qs_scr[rs, :], lambda rs: ks_scr[rs, :], lambda rs: v_ref[0, rs, :])]
    for g in range(1, len(DIL_GROUPS)):
        d_prev, d_cur = DIL_GROUPS[g - 1][1], DIL_GROUPS[g][1]
        ratio = d_cur // d_prev
        len_prev, len_cur = seq // d_prev, seq // d_cur
        for t in range(3):
            for c in range(d_prev):
                for m in range(ratio):
                    src = pl.ds(c * len_prev + m, len_cur, stride=ratio)
                    perm_scr[g - 1, t, pl.ds((c + d_prev * m) * len_cur, len_cur), :] = stage_rows[g - 1][t](src)
        stage_rows.append(tuple((lambda rs, g=g, t=t: perm_scr[g - 1, t, rs, :]) for t in range(3)))

    for g, (_, dil) in enumerate(DIL_GROUPS):
        n_blk = seq // dil // BAND
        has_prev = n_blk > 1
        q_rows, k_rows, v_rows = stage_rows[g]

        def rows(start, dil=dil):
            return pl.ds(start, BAND, stride=dil) if dil > 1 else pl.ds(start, BAND)

        def body(trip, carry, g=g, dil=dil, n_blk=n_blk, rows=rows, has_prev=has_prev,
                 q_rows=q_rows, k_rows=k_rows, v_rows=v_rows):
            blocks = []
            for u in range(ATTN_PAR):
                idx = trip * ATTN_PAR + u
                r = idx // n_blk
                nb = idx % n_blk
                cur = pl.ds(pl.multiple_of((r * n_blk + nb) * BAND, BAND), BAND)
                qb = q_rows(cur)
                kcat = k_rows(cur)
                vcat = v_rows(cur)
                bias = [bias_ref[g, h, :, BAND:] for h in range(2)]
                if has_prev:
                    prev = pl.ds(pl.multiple_of((r * n_blk + jnp.maximum(nb - 1, 0)) * BAND, BAND), BAND)
                    kcat = jnp.concatenate([k_rows(prev), kcat], axis=0)
                    vcat = jnp.concatenate([v_rows(prev), vcat], axis=0)
                    no_prev = jnp.logical_and(nb == 0, col < BAND)
                    bias = [jnp.where(no_prev, NEG, bias_ref[g, h]) for h in range(2)]
                blocks.append(dict(out=rows(r + nb * (BAND * dil)), q=qb, k=kcat.astype(BF16),
                                   v=vcat.astype(BF16), bias=bias))
            heads = [(blk, h) for blk in blocks for h in range(2)]
            s = [_dot_nt(jnp.where(lo if h == 0 else jnp.logical_not(lo), blk["q"], 0.0).astype(BF16), blk["k"])
                 + blk["bias"][h] for blk, h in heads]
            mx = [jnp.max(x, axis=-1, keepdims=True) for x in s]
            p = [jnp.exp(x - m) for x, m in zip(s, mx)]
            sm = [jnp.sum(x, axis=-1, keepdims=True) for x in p]
            o = [_dot(x.astype(BF16), blk["v"]) for x, (blk, _) in zip(p, heads)]
            for u, blk in enumerate(blocks):
                m_scr[g, blk["out"], :] = jnp.where(lo, mx[2 * u], mx[2 * u + 1])
                l_scr[g, blk["out"], :] = jnp.where(lo, sm[2 * u], sm[2 * u + 1])
                acc_scr[g, blk["out"], :] = jnp.where(lo, o[2 * u], o[2 * u + 1])
            return carry

        lax.fori_loop(0, seq // BAND // ATTN_PAR, body, 0)

    n_g = len(DIL_GROUPS)
    m_all = m_scr[0]
    for g in range(1, n_g):
        m_all = jnp.maximum(m_all, m_scr[g])
    num = jnp.zeros((seq, LANES), F32)
    den = jnp.zeros((seq, LANES), F32)
    for g in range(n_g):
        w = jnp.exp(m_scr[g] - m_all)
        num = num + w * acc_scr[g]
        den = den + w * l_scr[g]
    oa_ref[0] = _head_pair_norm(num / den, og_ref[...], same_head).astype(BF16)


def _attn_prompt(l, proj, bias_tiles, qg, kg, og, weights, stacked):
    B, S, _ = proj.shape
    assert bool(stacked) == (l > 0)
    n_pair = MIX_A // LANES
    n_g = len(DIL_GROUPS)
    blk = lambda off: pl.BlockSpec((1, S, LANES), lambda b, hp: (b, 0, off + hp))
    vec = pl.BlockSpec((1, LANES), lambda b, hp: (0, 0))
    if stacked:
        t_blk = pl.BlockSpec((None, 1, 2, HEAD_DIM, S), lambda b, hp: (l, b, hp, 0, 0))
    else:
        t_blk = pl.BlockSpec((DEPTH, 1, 2, HEAD_DIM, S), lambda b, hp: (0, b, hp, 0, 0))
    w_tile = lambda w: (w.shape[1] // B, w.shape[2] // n_pair)
    w_in_specs = [pl.BlockSpec((None,) + w_tile(w), lambda b, hp: (l, b, hp)) for w in weights]
    w_out_specs = [pl.BlockSpec((None,) + w_tile(w), lambda b, hp: (0, b, hp)) for w in weights]
    n_in = 7 + len(weights)
    return pl.pallas_call(
        functools.partial(_attn_p_kernel, seq=S, n_alias=len(stacked)),
        grid=(B, n_pair),
        in_specs=[blk(0), blk(n_pair), blk(2 * n_pair),
                  pl.BlockSpec((n_g, 2, BAND, 2 * BAND), lambda b, hp: (0, hp, 0, 0)),
                  vec, vec, vec] + w_in_specs + [pl.BlockSpec(memory_space=pl.ANY)] * len(stacked),
        out_specs=[t_blk, t_blk, pl.BlockSpec((1, S, LANES), lambda b, hp: (b, 0, hp))] + w_out_specs,
        out_shape=[jax.ShapeDtypeStruct((DEPTH, B, H_A, HEAD_DIM, S), F32),
                   jax.ShapeDtypeStruct((DEPTH, B, H_A, HEAD_DIM, S), F32),
                   jax.ShapeDtypeStruct((B, S, MIX_A), BF16)]
                  + [jax.ShapeDtypeStruct((1,) + w.shape[1:], BF16) for w in weights],
        input_output_aliases={n_in + i: i for i in range(len(stacked))},
        scratch_shapes=([pltpu.VMEM((S, LANES), F32)] * 2 + [pltpu.VMEM((n_g - 1, 3, S, LANES), F32)]
                        + [pltpu.VMEM((n_g, S, LANES), F32)] * 3),
        compiler_params=_cparams("parallel", "parallel"),
        name="attn_prompt",
    )(proj, proj, proj, bias_tiles, qg, kg, og, *weights, *stacked)


def _gla_p_kernel(q_ref, k_ref, v_ref, la_ref, r_ref, g_ref, ob_ref, st_ref, st_scr, *, seq):
    C = GLA_BLOCK
    n_sub = C // GLA_SUB
    st_scr[...] = jnp.zeros(st_scr.shape, F32)
    row = lax.broadcasted_iota(jnp.int32, (C, C), 0)
    tri = row >= lax.broadcasted_iota(jnp.int32, (C, C), 1)
    tri_b = jnp.where(tri, 1.0, 0.0).astype(BF16)
    krow = lax.broadcasted_iota(jnp.int32, (C, DK_B), 0)

    heads = range(GLA_PAR)
    ksl = [slice(DK_B * h, DK_B * (h + 1)) for h in heads]
    vsl = [slice(DV_B * h, DV_B * (h + 1)) for h in heads]

    def block(c, carry):
        rs = pl.ds(pl.multiple_of(c * C, C), C)
        la = [la_ref[0, rs, ksl[h]] for h in heads]
        hi = [x.astype(BF16) for x in la]
        r1 = [x - y.astype(F32) for x, y in zip(la, hi)]
        mid = [x.astype(BF16) for x in r1]
        low = [(x - y.astype(F32)).astype(BF16) for x, y in zip(r1, mid)]
        cum = [_dot(tri_b, a) + _dot(tri_b, b) + _dot(tri_b, d) for a, b, d in zip(hi, mid, low)]
        q = [q_ref[0, rs, ksl[h]] for h in heads]
        k = [k_ref[0, rs, ksl[h]] for h in heads]
        v = [v_ref[0, rs, vsl[h]] for h in heads]
        starts = [[(cum[h] - la[h])[GLA_SUB * i:GLA_SUB * i + 1, :] for i in range(n_sub)] for h in heads]
        start_rows = [jnp.concatenate([jnp.broadcast_to(s, (GLA_SUB, DK_B)) for s in starts[h]], axis=0)
                      for h in heads]
        qe = [(q[h] * jnp.exp(cum[h] - start_rows[h])).astype(BF16) for h in heads]
        khat = [[(k[h] * jnp.exp(jnp.where(krow < GLA_SUB * (i + 1), starts[h][i] - cum[h], NEG))).astype(BF16)
                 for i in range(n_sub)] for h in heads]
        att = [jnp.concatenate([_dot_nt(qe[h][GLA_SUB * i:GLA_SUB * (i + 1)], khat[h][i]) for i in range(n_sub)],
                               axis=0) for h in heads]
        att = [jnp.where(tri, x, 0.0).astype(BF16) for x in att]
        o = [_dot(att[h], v[h].astype(BF16))
             + _dot_nt((q[h] * jnp.exp(cum[h])).astype(BF16), st_scr[h].astype(BF16)) for h in heads]
        last = [x[C - 1:C, :] for x in cum]
        kd = [(k[h] * jnp.exp(last[h] - cum[h])).astype(BF16) for h in heads]
        upd = [_dot(v[h].T.astype(BF16), kd[h]) for h in heads]
        for h in heads:
            st_scr[h] = st_scr[h] * jnp.exp(last[h]) + upd[h]
            ob_ref[0, rs, vsl[h]] = (_rms(o[h], g_ref[...]) * _silu(r_ref[0, rs, vsl[h]])).astype(BF16)
        return carry

    lax.fori_loop(0, seq // C, block, 0, unroll=GLA_UNROLL)
    for h in heads:
        st_ref[0, h] = st_scr[h].T


def _gla_prompt(proj, log_a, gla_g):
    B, S, _ = proj.shape
    kw, vw = GLA_PAR * DK_B, GLA_PAR * DV_B
    q0 = 3 * MIX_A // kw
    k0 = q0 + KEY_B // kw
    v0 = (3 * MIX_A + 2 * KEY_B) // vw
    r0 = v0 + MIX_B // vw
    kblk = lambda off: pl.BlockSpec((1, S, kw), lambda b, h: (b, 0, off + h))
    vblk = lambda off: pl.BlockSpec((1, S, vw), lambda b, h: (b, 0, off + h))
    return pl.pallas_call(
        functools.partial(_gla_p_kernel, seq=S),
        grid=(B, H_B // GLA_PAR),
        in_specs=[kblk(q0), kblk(k0), vblk(v0), kblk(0), vblk(r0),
                  pl.BlockSpec((1, DV_B), lambda b, h: (0, 0))],
        out_specs=[vblk(0), pl.BlockSpec((1, GLA_PAR, DK_B, DV_B), lambda b, h: (b, h, 0, 0))],
        out_shape=[jax.ShapeDtypeStruct((B, S, MIX_B), BF16),
                   jax.ShapeDtypeStruct((B, H_B, DK_B, DV_B), F32)],
        scratch_shapes=[pltpu.VMEM((GLA_PAR, DV_B, DK_B), F32)],
        compiler_params=_cparams("parallel", "parallel"),
        name="gla_prompt",
    )(proj, proj, proj, log_a, proj, gla_g)


def _gla_s_kernel(q_ref, k_ref, v_ref, la_ref, r_ref, g_ref, s_ref, ob_ref, so_ref, *, steps):
    rowi = lax.broadcasted_iota(jnp.int32, (DK_B, DK_B), 0)
    heads = range(H_B)
    ks = [slice(DK_B * h, DK_B * (h + 1)) for h in heads]
    vs = [slice(DV_B * h, DV_B * (h + 1)) for h in heads]
    x = [jnp.zeros((DK_B, DK_B), F32) for _ in heads]
    for t in range(steps):
        x = [jnp.where(rowi == t, k_ref[0, t:t + 1, ks[h]], x[h]) for h in heads]
        x = [jnp.where(rowi == steps + t, jnp.exp(la_ref[0, t:t + 1, ks[h]]), x[h]) for h in heads]
        x = [jnp.where(rowi == 2 * steps + t, q_ref[0, t:t + 1, ks[h]], x[h]) for h in heads]
    xt = [a.T for a in x]
    st = [s_ref[0, h] for h in heads]
    for t in range(steps):
        st = [xt[h][:, steps + t:steps + t + 1] * st[h] + xt[h][:, t:t + 1] * v_ref[0, t:t + 1, vs[h]]
              for h in heads]
        o = [jnp.sum(xt[h][:, 2 * steps + t:2 * steps + t + 1] * st[h], axis=0, keepdims=True) for h in heads]
        for h in heads:
            ob_ref[0, t:t + 1, vs[h]] = _rms(o[h], g_ref[...]) * _silu(r_ref[0, t:t + 1, vs[h]])
    for h in heads:
        so_ref[0, h] = st[h]


def _gla_sample(l, q, k, v, log_a, r, gla_g, state):
    B, T, _ = q.shape
    kb = pl.BlockSpec((1, T, KEY_B), lambda b: (b, 0, 0))
    vb = pl.BlockSpec((1, T, MIX_B), lambda b: (b, 0, 0))
    return pl.pallas_call(
        functools.partial(_gla_s_kernel, steps=T),
        grid=(B,),
        in_specs=[kb, kb, vb, kb, vb, pl.BlockSpec((1, DV_B), lambda b: (0, 0)),
                  pl.BlockSpec((None, 1, H_B, DK_B, DV_B), lambda b: (l, b, 0, 0, 0))],
        out_specs=[vb, pl.BlockSpec((1, H_B, DK_B, DV_B), lambda b: (b, 0, 0, 0))],
        out_shape=[jax.ShapeDtypeStruct((B, T, MIX_B), F32),
                   jax.ShapeDtypeStruct((B, H_B, DK_B, DV_B), F32)],
        compiler_params=_cparams("parallel"),
        name="gla_sample",
    )(q, k, v, log_a, r, gla_g, state)


def _attn_s_kernel(q_ref, k_ref, v_ref, kt_ref, vt_ref, bnear_ref, bfar_ref, bnew_ref,
                   qg_ref, kg_ref, og_ref, kn_ref, oa_ref, kpad_scr, vpad_scr, *, n_near):
    n_head, n_row, _ = kn_ref.shape[1:]
    width = kt_ref.shape[-1]
    kn = _rms(k_ref[0], kg_ref[...])
    kn_ref[0] = kn
    qs = _rms(q_ref[0], qg_ref[...]) * ATTN_SCALE
    kpad_scr[...] = jnp.zeros(kpad_scr.shape, F32)
    vpad_scr[...] = jnp.zeros(vpad_scr.shape, F32)
    kpad_scr[:, 0:n_row, :] = kn
    vpad_scr[:, 0:n_row, :] = v_ref[0]
    n_g = len(DIL_GROUPS)
    for h0 in range(0, n_head, ATTN_S_PAR):
        hs = range(h0, min(h0 + ATTN_S_PAR, n_head))
        qh = [qs[h].astype(BF16) for h in hs]
        s = [_dot(q, kt_ref[h].astype(BF16)) for q, h in zip(qh, hs)]
        s_new = [_dot_nt(q, kpad_scr[h].astype(BF16)) for q, h in zip(qh, hs)]
        far = [x + bfar_ref[h] for x, h in zip(s, hs)]
        near = [[x[:, width - n_near:] + bnear_ref[g, h] for g in range(n_g - 1)] for x, h in zip(s, hs)]
        new = [[x + bnew_ref[g, h] for g in range(n_g)] for x, h in zip(s_new, hs)]
        m = [jnp.max(x, axis=-1, keepdims=True) for x in far]
        for i in range(len(hs)):
            for x in near[i] + new[i]:
                m[i] = jnp.maximum(m[i], jnp.max(x, axis=-1, keepdims=True))
        p_far = [jnp.exp(x - mm) for x, mm in zip(far, m)]
        p_near = [sum(jnp.exp(x - mm) for x in xs) for xs, mm in zip(near, m)]
        p_new = [sum(jnp.exp(x - mm) for x in xs) for xs, mm in zip(new, m)]
        den = [jnp.sum(a, axis=-1, keepdims=True) + jnp.sum(b, axis=-1, keepdims=True)
               + jnp.sum(c, axis=-1, keepdims=True) for a, b, c in zip(p_far, p_near, p_new)]
        p = [jnp.concatenate([a[:, :width - n_near], a[:, width - n_near:] + b], axis=1)
             for a, b in zip(p_far, p_near)]
        o = [_dot_nt(x.astype(BF16), vt_ref[h].astype(BF16)) + _dot(y.astype(BF16), vpad_scr[h].astype(BF16))
             for x, y, h in zip(p, p_new, hs)]
        for i, h in enumerate(hs):
            oa_ref[0, h] = _rms(o[i] / den[i], og_ref[...])


def _attn_sample(l, qkv, cache_kt, cache_vt, bias_near, bias_far, bias_new, qg, kg, og):
    _, B, n_head, n_row, _ = qkv.shape
    W = cache_kt.shape[-1]
    n_near = bias_near.shape[-1]
    qblk = lambda i: pl.BlockSpec((None, 1, n_head, n_row, HEAD_DIM), lambda b: (i, b, 0, 0, 0))
    cblk = pl.BlockSpec((None, None, n_head, HEAD_DIM, W), lambda b: (l, b, 0, 0, 0))
    full = lambda a: pl.BlockSpec(a.shape, lambda b: (0,) * a.ndim)
    vec = pl.BlockSpec((1, HEAD_DIM), lambda b: (0, 0))
    out_blk = pl.BlockSpec((1, n_head, n_row, HEAD_DIM), lambda b: (b, 0, 0, 0))
    return pl.pallas_call(
        functools.partial(_attn_s_kernel, n_near=n_near),
        grid=(B,),
        in_specs=[qblk(0), qblk(1), qblk(2), cblk, cblk,
                  full(bias_near), full(bias_far), full(bias_new), vec, vec, vec],
        out_specs=[out_blk, out_blk],
        out_shape=[jax.ShapeDtypeStruct((B, n_head, n_row, HEAD_DIM), F32),
                   jax.ShapeDtypeStruct((B, n_head, n_row, HEAD_DIM), F32)],
        scratch_shapes=[pltpu.VMEM((n_head, BAND, HEAD_DIM), F32)] * 2,
        compiler_params=_cparams("parallel"),
        name="attn_sample",
    )(qkv, qkv, qkv, cache_kt, cache_vt, bias_near, bias_far, bias_new, qg, kg, og)


def _bucket_table(n):
    dist = np.arange(n)
    exact = N_BUCKETS // 2
    df = np.maximum(dist, 1).astype(np.float32)
    large = exact + (np.log(df / np.float32(exact)) / np.float32(math.log(MAX_DISTANCE / exact))
                     * np.float32(N_BUCKETS - exact)).astype(np.int32)
    return np.where(dist < exact, dist, np.minimum(large, N_BUCKETS - 1))


def _prompt_bias_tiles(rel_bias):
    span = 3 * BAND
    tiles = []
    for _, dil in DIL_GROUPS:
        steps = np.arange(BAND, -1, -1)
        vec = rel_bias[_bucket_table(BAND * dil + 1)[steps * dil]].T.astype(F32)
        u = jnp.concatenate([vec, jnp.full((H_A, span - BAND - 1), NEG, F32)], axis=1)
        t = jnp.tile(u, (1, BAND))[:, :BAND * (span - 1)].reshape(H_A, BAND, span - 1)
        tiles.append(t[:, :, :2 * BAND])
    return jnp.stack(tiles)


def _sample_bias_tiles(rel_bias, steps, rows, wbuf):
    n_dist = wbuf + rows
    bucket = _bucket_table(n_dist)
    dist = np.arange(n_dist)
    tab = rel_bias[bucket].T.astype(F32)
    n_near = DIL_GROUPS[-2][0]
    tiles = []
    for win, dil in DIL_GROUPS:
        ok = (dist % dil == 0) & (dist >= dil) & (dist <= win)
        rev = jnp.where(ok[None, ::-1], tab[:, ::-1], NEG)
        tiles.append(jnp.stack([rev[:, rows - 1 - t:rows - 1 - t + wbuf] for t in range(rows)], axis=1))
    near = jnp.stack([t[:, :, wbuf - n_near:] for t in tiles[:-1]])
    far = tiles[-1]
    t_new = np.arange(rows)[:, None]
    tau = np.arange(BAND)[None, :]
    live = (t_new < steps) & (tau < steps)
    back = np.clip(t_new - tau, 0, steps)
    new = []
    for g, (_, dil) in enumerate(DIL_GROUPS):
        ok = live & ((tau <= t_new) if g == 0 else (tau == t_new))
        new.append(jnp.where(ok[None], rel_bias[bucket[back * dil]].transpose(2, 0, 1).astype(F32), NEG))
    return near, far, jnp.stack(new)


def kernel(x_prompt, x_sample, cache_k_win, cache_v_win, state_gla, state_conv, c_prompt, c_sample,
           rel_bias, norm1_g, norm2_g, w_ada, b_ada, w_in, q_norm_g, k_norm_g, w_gate2, b_gate2,
           a_out_g, gla_out_g, w_out, w_up, conv_w, conv_b, w_down):
    B, S, _ = x_prompt.shape
    Bs, T, _ = x_sample.shape
    wbuf = cache_k_win.shape[2]
    Mp, Ms = B * S, Bs * T
    assert wbuf == DIL_GROUPS[-1][0] and T <= SUBLANES

    w_in_t = jnp.transpose(w_in, (0, 2, 1))
    w_main = _main_cols_bf16(w_in_t)
    w_glr = jnp.pad(w_in_t[:, MAIN_COLS:], ((0, 0), (0, LANES - GATE_RANK), (0, 0))).astype(BF16)
    w_g2 = jnp.pad(w_gate2, ((0, 0), (0, LANES - GATE_RANK), (0, 0))).astype(BF16)
    b_g2 = b_gate2.reshape(DEPTH, 1, KEY_B)
    col_scale = jnp.ones((1, MAIN_COLS), F32).at[:, 3 * MIX_A:3 * MIX_A + KEY_B].set(DK_B ** -0.5)
    vec3 = lambda a: a.reshape(DEPTH, 1, -1)
    pair = lambda a, l: jnp.tile(a[l], LANES // HEAD_DIM).reshape(1, LANES)
    head = lambda a, l: a[l].reshape(1, HEAD_DIM)

    mods = _ada_mods(jnp.concatenate([c_prompt, c_sample], axis=0), w_ada, b_ada)

    bias_p = _prompt_bias_tiles(rel_bias)
    bias_near, bias_far, bias_new = _sample_bias_tiles(rel_bias, T, SUBLANES, wbuf)

    cache_kt = jnp.transpose(cache_k_win, (0, 1, 3, 4, 2))
    cache_vt = jnp.transpose(cache_v_win, (0, 1, 3, 4, 2))
    hist_p = jnp.zeros((B, CONV_W - 1, 1, D_FF), F32)

    xp = x_prompt.reshape(Mp, D_MODEL)
    xs = jnp.transpose(x_sample, (1, 0, 2)).reshape(Ms, D_MODEL)
    outs = [[] for _ in range(6)]
    kv_p = ()
    for l in range(DEPTH):
        mod_p = [m.reshape(B, 1, D_MODEL) for m in jnp.split(mods[l, :B], 6, axis=-1)]
        mod_s = [jnp.tile(m, (T, 1)).reshape(1, Ms, D_MODEL) for m in jnp.split(mods[l, B:], 6, axis=-1)]

        proj, log_a = _in_proj(l, xp, vec3(norm1_g), mod_p[1], mod_p[0], w_main, col_scale, w_glr, w_g2, b_g2,
                               tm=TM_IN, rows_per_group=S)
        proj = proj.reshape(B, S, MAIN_COLS)
        knt, vt, oa, w_out_b, w_up_b, w_down_b = _attn_prompt(
            l, proj, bias_p, pair(q_norm_g, l), pair(k_norm_g, l), pair(a_out_g, l), (w_out, w_up, w_down), kv_p)
        kv_p = (knt, vt)
        ob, g_st = _gla_prompt(proj, log_a.reshape(B, S, KEY_B), gla_out_g[l].reshape(1, DV_B))
        xp, h2 = _out_proj(l, oa.reshape(Mp, MIX_A), ob.reshape(Mp, MIX_B), xp, mod_p[2], w_out_b,
                           vec3(norm2_g), mod_p[4], mod_p[3], tm=TM_OUT, rows_per_group=S)
        act, tail = _up_proj(l, h2, w_up_b, conv_w, vec3(conv_b), hist_p, n_seq=1, steps=S, tn=TN_UP)
        xp = _down_proj(act, w_down_b, xp, mod_p[5], tm=TM_DOWN, tn=TN_DOWN, rows_per_group=S)
        outs[0].append(g_st)
        outs[1].append(tail[:, tail.shape[1] - (CONV_W - 1):])

        proj, log_a = _in_proj(l, xs, vec3(norm1_g), mod_s[1], mod_s[0], w_main, col_scale, w_glr, w_g2, b_g2,
                               tm=Ms, rows_per_group=Ms)
        bm = lambda a: jnp.transpose(a.reshape(T, Bs, -1), (1, 0, 2))
        qkv = jnp.transpose(proj[:, :3 * MIX_A].reshape(T, Bs, 3, H_A, HEAD_DIM), (2, 1, 3, 0, 4))
        qkv_pad = jnp.pad(qkv, ((0, 0), (0, 0), (0, 0), (0, SUBLANES - T), (0, 0)))
        kn_s, oa_s = _attn_sample(l, qkv_pad, cache_kt, cache_vt, bias_near, bias_far, bias_new,
                                  head(q_norm_g, l), head(k_norm_g, l), head(a_out_g, l))
        gla_in = bm(proj[:, 3 * MIX_A:])
        ob_s, g_st = _gla_sample(l, gla_in[..., :KEY_B], gla_in[..., KEY_B:2 * KEY_B],
                                 gla_in[..., 2 * KEY_B:2 * KEY_B + MIX_B], bm(log_a),
                                 gla_in[..., 2 * KEY_B + MIX_B:], gla_out_g[l].reshape(1, DV_B), state_gla)
        oa_t = jnp.transpose(oa_s[:, :, :T], (2, 0, 1, 3)).reshape(Ms, MIX_A).astype(BF16)
        ob_t = jnp.transpose(ob_s, (1, 0, 2)).reshape(Ms, MIX_B).astype(BF16)
        xs, h2 = _out_proj(l, oa_t, ob_t, xs, mod_s[2], w_out_b,
                           vec3(norm2_g), mod_s[4], mod_s[3], tm=Ms, rows_per_group=Ms)
        hist_s = jnp.transpose(state_conv[l], (1, 0, 2)).reshape(1, CONV_W - 1, Bs, D_FF)
        act, tail = _up_proj(l, h2, w_up_b, conv_w, vec3(conv_b), hist_s, n_seq=Bs, steps=T, tn=TN_UP)
        xs = _down_proj(act, w_down_b, xs, mod_s[5], tm=Ms, tn=TN_DOWN, rows_per_group=Ms)
        outs[2].append(jnp.transpose(kn_s[:, :, :T], (0, 2, 1, 3)))
        outs[3].append(jnp.transpose(qkv[2], (0, 2, 1, 3)))
        outs[4].append(g_st)
        outs[5].append(jnp.transpose(tail.reshape(CONV_W - 1, Bs, D_FF), (1, 0, 2)))

    y_prompt = xp.reshape(B, S, D_MODEL)
    y_sample = jnp.transpose(xs.reshape(T, Bs, D_MODEL), (1, 0, 2))
    kv_prompt = tuple(jnp.transpose(a, (0, 1, 4, 2, 3)) for a in kv_p)
    return (y_prompt, y_sample) + kv_prompt + tuple(jnp.stack(o) for o in outs)
```
